```python
import jax, jax.numpy as jnp
from jax import lax
import numpy as np

D_MODEL = 1024
BATCH = 16
SEQ = 2048
DEPTH = 4
DEC_BATCH = 32
DEC_SEQ = 16
PAST_LEN = 2048

CHUNK = 64
N_A_LAYERS = DEPTH // 2
N_B_LAYERS = DEPTH - N_A_LAYERS
POOL_WINDOWS = (2, 4, 8, 16)
N_POOL_GROUPS = len(POOL_WINDOWS)
POOL_GROUP_DIM = D_MODEL // N_POOL_GROUPS
POOL_HIST = max(POOL_WINDOWS) - 1
N_HEADS = 16
HEAD_DIM = D_MODEL // N_HEADS
N_KV_HEADS = 4
KV_GROUP = N_HEADS // N_KV_HEADS
Q_WIDTH = N_HEADS * HEAD_DIM
KV_WIDTH = N_KV_HEADS * HEAD_DIM
KV_PROJ_WIDTH = 2 * KV_WIDTH + N_HEADS
Q_BLOCK = 128
N_EXPERTS = 32
TOP_K = 4
D_FF = D_MODEL
SWIGLU_LIMIT = 7.0
SWIGLU_ALPHA = 1.702
EXPERT_BLOCK = 256
RMS_EPS = 1e-6
NEG_INF = -1e30

kernel_name = 'yoco_pool_fox_moe_stream_step'


def _rms_norm(x, g):
    xf = x.astype(jnp.float32)
    y = xf * lax.rsqrt(jnp.mean(xf * xf, axis=-1, keepdims=True) + RMS_EPS)
    return (y * g.astype(jnp.float32)).astype(x.dtype)


def _modulate(x, shift, scale):
    return x * (1 + scale) + shift


def _ada(c, w, b, n):
    m = jax.nn.silu(c) @ w + b
    return [t[:, None, :] for t in jnp.split(m, n, axis=-1)]


def _pool_mixer(h, hist, w_pool, pool_scale, pos0):
    bsz, seq_len, _ = h.shape
    ext = jnp.concatenate([hist.astype(h.dtype), h], axis=1)
    extf = ext.astype(jnp.float32)
    cs = jnp.concatenate([jnp.zeros((bsz, 1, D_MODEL), jnp.float32), jnp.cumsum(extf, axis=1)], axis=1)
    pos = pos0 + jnp.arange(seq_len, dtype=jnp.int32)
    outs = []
    for g, w in enumerate(POOL_WINDOWS):
        lo, hi = g * POOL_GROUP_DIM, (g + 1) * POOL_GROUP_DIM
        win = cs[:, POOL_HIST + 1:POOL_HIST + 1 + seq_len, lo:hi] - cs[:, POOL_HIST + 1 - w:POOL_HIST + 1 - w + seq_len, lo:hi]
        cnt = jnp.minimum(pos + 1, w).astype(jnp.float32)
        outs.append(win / cnt[None, :, None])
    pooled = jnp.concatenate(outs, axis=-1).astype(h.dtype)
    u = (pooled - h).reshape(bsz, seq_len, N_POOL_GROUPS, POOL_GROUP_DIM)
    y = jnp.einsum('btgc,gcd->btgd', u, w_pool).reshape(bsz, seq_len, D_MODEL)
    return y * pool_scale, ext[:, -POOL_HIST:]


def _forget_attention(q, k, v, cum_q, cum_k, pos_q, pos_k):
    bsz, tq = q.shape[:2]
    tk = k.shape[1]
    scale = HEAD_DIM ** -0.5
    qg = q.reshape(bsz, tq, N_KV_HEADS, KV_GROUP, HEAD_DIM)
    cq = cum_q.reshape(bsz, tq, N_KV_HEADS, KV_GROUP)
    ck = cum_k.reshape(bsz, tk, N_KV_HEADS, KV_GROUP).transpose(0, 2, 3, 1)

    def block(args):
        qb, cqb, pqb = args
        s = jnp.einsum('bqkgd,bskd->bkgqs', qb, k, preferred_element_type=jnp.float32) * scale
        s = s + cqb.astype(jnp.float32).transpose(0, 2, 3, 1)[..., :, None] - ck[..., None, :]
        mask = pos_k[None, :] <= pqb[:, None]
        s = jnp.where(mask, s, NEG_INF)
        p = jax.nn.softmax(s, axis=-1)
        return jnp.einsum('bkgqs,bskd->bqkgd', p.astype(v.dtype), v)

    if tq <= Q_BLOCK:
        o = block((qg, cq, pos_q))
    else:
        nb = tq // Q_BLOCK
        qs = jnp.moveaxis(qg.reshape(bsz, nb, Q_BLOCK, N_KV_HEADS, KV_GROUP, HEAD_DIM), 1, 0)
        cs = jnp.moveaxis(cq.reshape(bsz, nb, Q_BLOCK, N_KV_HEADS, KV_GROUP), 1, 0)
        ps = pos_q.reshape(nb, Q_BLOCK)
        o = lax.map(block, (qs, cs, ps))
        o = jnp.moveaxis(o, 0, 1)
    return o.reshape(bsz, tq, N_HEADS, HEAD_DIM)


def _moe(h, w_router, b_router, w_gate_up, b_gate_up, w_down, b_down):
    bsz, seq_len, _ = h.shape
    n_tok = bsz * seq_len
    ht = h.reshape(n_tok, D_MODEL)
    logits = jnp.einsum('td,de->te', ht, w_router, preferred_element_type=jnp.float32) + b_router.astype(jnp.float32)
    top_logit, top_idx = lax.top_k(logits, TOP_K)
    gates = jax.nn.softmax(top_logit, axis=-1).astype(h.dtype)
    n_asg = n_tok * TOP_K
    flat_e = top_idx.reshape(-1).astype(jnp.int32)
    order = jnp.argsort(flat_e)
    sorted_e = flat_e[order]
    sorted_tok = (order // TOP_K).astype(jnp.int32)
    sorted_gate = gates.reshape(-1)[order]
    counts = jnp.zeros((N_EXPERTS,), jnp.int32).at[flat_e].add(1)
    padded = (counts + EXPERT_BLOCK - 1) // EXPERT_BLOCK * EXPERT_BLOCK
    pad_end = jnp.cumsum(padded)
    pad_start = pad_end - padded
    start = jnp.cumsum(counts) - counts
    dest = pad_start[sorted_e] + jnp.arange(n_asg, dtype=jnp.int32) - start[sorted_e]
    n_blocks = -(-n_asg // EXPERT_BLOCK) + N_EXPERTS
    n_rows = n_blocks * EXPERT_BLOCK
    row_tok = jnp.full((n_rows,), n_tok, jnp.int32).at[dest].set(sorted_tok)
    row_gate = jnp.zeros((n_rows,), h.dtype).at[dest].set(sorted_gate)
    blk_start = jnp.arange(n_blocks, dtype=jnp.int32) * EXPERT_BLOCK
    blk_expert = jnp.minimum(jnp.sum(pad_end[None, :] <= blk_start[:, None], axis=1), N_EXPERTS - 1).astype(jnp.int32)
    ht_pad = jnp.concatenate([ht, jnp.zeros((1, D_MODEL), ht.dtype)], axis=0)

    def run_block(args):
        tok, e = args
        xb = ht_pad[tok]
        gu = xb @ w_gate_up[e] + b_gate_up[e]
        glu = jnp.minimum(gu[:, :D_FF], SWIGLU_LIMIT)
        lin = jnp.clip(gu[:, D_FF:], -SWIGLU_LIMIT, SWIGLU_LIMIT)
        act = glu * jax.nn.sigmoid(SWIGLU_ALPHA * glu) * (lin + 1)
        return act @ w_down[e] + b_down[e]

    out = lax.map(run_block, (row_tok.reshape(n_blocks, EXPERT_BLOCK), blk_expert))
    out = out.reshape(n_rows, D_MODEL) * row_gate[:, None]
    y = jnp.zeros((n_tok + 1, D_MODEL), out.dtype).at[row_tok].add(out)[:n_tok]
    return y.reshape(bsz, seq_len, D_MODEL)


def _trunk(x, c, pool_hist, kv_past, pos0, p):
    bsz, seq_len, _ = x.shape
    pos_q = pos0 + jnp.arange(seq_len, dtype=jnp.int32)
    if pool_hist is None:
        pool_hist = jnp.zeros((N_A_LAYERS, bsz, POOL_HIST, D_MODEL), x.dtype)
    new_pool = []
    k_all = v_all = cum_q = cum_k = pos_k = None
    k_new = v_new = logf_new = None
    for layer in range(DEPTH):
        sh1, sc1, g1, sh2, sc2, g2 = _ada(c, p['w_ada'][layer], p['b_ada'][layer], 6)
        h = _modulate(_rms_norm(x, p['norm_mix_g'][layer]), sh1, sc1)
        if layer < N_A_LAYERS:
            y, hist = _pool_mixer(h, pool_hist[layer], p['w_pool'][layer], p['pool_scale'][layer], pos0)
            new_pool.append(hist)
        else:
            j = layer - N_A_LAYERS
            q = (h @ p['w_q'][j]).reshape(bsz, seq_len, N_HEADS, HEAD_DIM)
            o = _forget_attention(q, k_all, v_all, cum_q, cum_k, pos_q, pos_k)
            y = o.reshape(bsz, seq_len, Q_WIDTH) @ p['w_o'][j]
        x = x + g1 * y
        h = _modulate(_rms_norm(x, p['norm_ffn_g'][layer]), sh2, sc2)
        x = x + g2 * _moe(h, p['w_router'][layer], p['b_router'][layer], p['w_gate_up'][layer],
                          p['b_gate_up'][layer], p['w_down'][layer], p['b_down'][layer])
        if layer == N_A_LAYERS - 1:
            shk, sck = _ada(c, p['w_ada_kv'], p['b_ada_kv'], 2)
            hkv = _modulate(_rms_norm(x, p['norm_kv_g']), shk, sck)
            kvf = hkv @ p['w_kv']
            k_new = kvf[..., :KV_WIDTH].reshape(bsz, seq_len, N_KV_HEADS, HEAD_DIM)
            v_new = kvf[..., KV_WIDTH:2 * KV_WIDTH].reshape(bsz, seq_len, N_KV_HEADS, HEAD_DIM)
            logf_new = jax.nn.log_sigmoid((kvf[..., 2 * KV_WIDTH:] + p['b_forget']).astype(jnp.float32))
            if kv_past is None:
                k_all, v_all, logf_all = k_new, v_new, logf_new
            else:
                ck_, cv_, clf_ = kv_past
                k_all = jnp.concatenate([ck_.astype(k_new.dtype), k_new], axis=1)
                v_all = jnp.concatenate([cv_.astype(v_new.dtype), v_new], axis=1)
                logf_all = jnp.concatenate([clf_.astype(jnp.float32), logf_new], axis=1)
            cum_k = jnp.cumsum(logf_all, axis=1)
            cum_q = cum_k[:, -seq_len:]
            pos_k = jnp.arange(k_all.shape[1], dtype=jnp.int32)
    shf, scf = _ada(c, p['w_ada_final'], p['b_ada_final'], 2)
    y = _modulate(_rms_norm(x, p['norm_final_g']), shf, scf)
    return y, k_new, v_new, logf_new, jnp.stack(new_pool, axis=0)


def setup_inputs(seed: int = 0) -> dict:
    key = jax.random.key(seed)
    ks = jax.random.split(key, 32)
    f32 = jnp.float32

    def nrm(k, shape, scale):
        return jax.random.normal(k, shape, f32) * scale

    inv = D_MODEL ** -0.5
    return {
        'x_prompt': nrm(ks[0], (BATCH, SEQ, D_MODEL), 1.0),
        'x_sample': nrm(ks[1], (DEC_BATCH, DEC_SEQ, D_MODEL), 1.0),
        'cache_k': nrm(ks[2], (DEC_BATCH, PAST_LEN, N_KV_HEADS, HEAD_DIM), 1.0),
        'cache_v': nrm(ks[3], (DEC_BATCH, PAST_LEN, N_KV_HEADS, HEAD_DIM), 1.0),
        'cache_logf': jax.nn.log_sigmoid(3.0 + nrm(ks[4], (DEC_BATCH, PAST_LEN, N_HEADS), 1.0)),
        'state_pool': nrm(ks[5], (N_A_LAYERS, DEC_BATCH, POOL_HIST, D_MODEL), 1.0),
        'c_prompt': nrm(ks[6], (BATCH, D_MODEL), 1.0),
        'c_sample': nrm(ks[7], (DEC_BATCH, D_MODEL), 1.0),
        'norm_mix_g': 1.0 + nrm(ks[8], (DEPTH, D_MODEL), 0.1),
        'norm_ffn_g': 1.0 + nrm(ks[9], (DEPTH, D_MODEL), 0.1),
        'w_ada': nrm(ks[10], (DEPTH, D_MODEL, 6 * D_MODEL), 0.5 * inv),
        'b_ada': nrm(ks[11], (DEPTH, 6 * D_MODEL), 0.02),
        'w_pool': nrm(ks[12], (N_A_LAYERS, N_POOL_GROUPS, POOL_GROUP_DIM, POOL_GROUP_DIM), POOL_GROUP_DIM ** -0.5),
        'pool_scale': 1.0 + nrm(ks[13], (N_A_LAYERS, D_MODEL), 0.1),
        'norm_kv_g': 1.0 + nrm(ks[14], (D_MODEL,), 0.1),
        'w_ada_kv': nrm(ks[15], (D_MODEL, 2 * D_MODEL), 0.5 * inv),
        'b_ada_kv': nrm(ks[16], (2 * D_MODEL,), 0.02),
        'w_kv': nrm(ks[17], (D_MODEL, KV_PROJ_WIDTH), inv),
        'b_forget': 3.0 + nrm(ks[18], (N_HEADS,), 0.1),
        'w_q': nrm(ks[19], (N_B_LAYERS, D_MODEL, Q_WIDTH), inv),
        'w_o': nrm(ks[20], (N_B_LAYERS, Q_WIDTH, D_MODEL), Q_WIDTH ** -0.5),
        'w_router': nrm(ks[21], (DEPTH, D_MODEL, N_EXPERTS), inv),
        'b_router': nrm(ks[22], (DEPTH, N_EXPERTS), 0.01),
        'w_gate_up': nrm(ks[23], (DEPTH, N_EXPERTS, D_MODEL, 2 * D_FF), inv),
        'b_gate_up': nrm(ks[24], (DEPTH, N_EXPERTS, 2 * D_FF), 0.02),
        'w_down': nrm(ks[25], (DEPTH, N_EXPERTS, D_FF, D_MODEL), D_FF ** -0.5),
        'b_down': nrm(ks[26], (DEPTH, N_EXPERTS, D_MODEL), 0.02),
        'norm_final_g': 1.0 + nrm(ks[27], (D_MODEL,), 0.1),
        'w_ada_final': nrm(ks[28], (D_MODEL, 2 * D_MODEL), 0.5 * inv),
        'b_ada_final': nrm(ks[29], (2 * D_MODEL,), 0.02),
    }


def reference(x_prompt, x_sample, cache_k, cache_v, cache_logf, state_pool, c_prompt, c_sample,
              norm_mix_g, norm_ffn_g, w_ada, b_ada, w_pool, pool_scale, norm_kv_g, w_ada_kv, b_ada_kv,
              w_kv, b_forget, w_q, w_o, w_router, b_router, w_gate_up, b_gate_up, w_down, b_down,
              norm_final_g, w_ada_final, b_ada_final):
    p = dict(norm_mix_g=norm_mix_g, norm_ffn_g=norm_ffn_g, w_ada=w_ada, b_ada=b_ada, w_pool=w_pool,
             pool_scale=pool_scale, norm_kv_g=norm_kv_g, w_ada_kv=w_ada_kv, b_ada_kv=b_ada_kv, w_kv=w_kv,
             b_forget=b_forget, w_q=w_q, w_o=w_o, w_router=w_router, b_router=b_router,
             w_gate_up=w_gate_up, b_gate_up=b_gate_up, w_down=w_down, b_down=b_down,
             norm_final_g=norm_final_g, w_ada_final=w_ada_final, b_ada_final=b_ada_final)
    y_prompt, k_prompt, v_prompt, logf_prompt, pool_prompt = _trunk(x_prompt, c_prompt, None, None, 0, p)
    y_sample, k_sample, v_sample, logf_sample, pool_sample = _trunk(
        x_sample, c_sample, state_pool, (cache_k, cache_v, cache_logf), PAST_LEN, p)
    return (y_prompt, y_sample, k_prompt, v_prompt, logf_prompt, pool_prompt,
            k_sample, v_sample, logf_sample, pool_sample)
```

```python
import functools

import jax
import jax.numpy as jnp
from jax import lax
from jax.experimental import pallas as pl
from jax.experimental.pallas import tpu as pltpu

F32 = jnp.float32
BF16 = jnp.bfloat16
HIGHEST = lax.Precision.HIGHEST

D_MODEL = 1024
DEPTH = 4
N_A_LAYERS = DEPTH // 2
POOL_WINDOWS = (2, 4, 8, 16)
POOL_GROUP_DIM = D_MODEL // len(POOL_WINDOWS)
POOL_HIST = max(POOL_WINDOWS) - 1
HIST_ROWS = POOL_HIST + 1
N_HEADS = 16
HEAD_DIM = D_MODEL // N_HEADS
N_KV_HEADS = 4
KV_GROUP = N_HEADS // N_KV_HEADS
KV_WIDTH = N_KV_HEADS * HEAD_DIM
N_EXPERTS = 32
TOP_K = 4
D_FF = D_MODEL
SWIGLU_LIMIT = 7.0
SWIGLU_ALPHA = 1.702
RMS_EPS = 1e-6
NEG_INF = -1e30

LANES = 128
TOKEN_TILE = 512
MOE_TILE = 256
FF_CHUNK = 512
ATTN_TQ = 256
ATTN_TK = 256
VMEM_LIMIT = 56 * 1024 * 1024


def _params(n_grid_dims):
    return pltpu.CompilerParams(dimension_semantics=("arbitrary",) * n_grid_dims, vmem_limit_bytes=VMEM_LIMIT)


def _norm_mod(x, gain, shift, scale):
    y = x * lax.rsqrt(jnp.mean(x * x, axis=-1, keepdims=True) + RMS_EPS) * gain
    return y * (1.0 + scale) + shift


def _ada_kernel(c_ref, w_ref, b_ref, o_ref):
    c = c_ref[...]
    s = c * jax.nn.sigmoid(c)
    o_ref[...] = jnp.dot(s, w_ref[...], preferred_element_type=F32, precision=HIGHEST) + b_ref[...]


def _ada_call(c, w, b):
    n_layers, _, m = w.shape
    bc = c.shape[0]
    tn = 1024
    return pl.pallas_call(
        _ada_kernel,
        grid=(n_layers, m // tn),
        in_specs=[pl.BlockSpec((bc, D_MODEL), lambda l, j: (0, 0)),
                  pl.BlockSpec((None, D_MODEL, tn), lambda l, j: (l, 0, j)),
                  pl.BlockSpec((None, 1, tn), lambda l, j: (l, 0, j))],
        out_specs=pl.BlockSpec((None, bc, tn), lambda l, j: (l, 0, j)),
        out_shape=jax.ShapeDtypeStruct((n_layers, bc, m), F32),
        compiler_params=_params(2),
        name="ada",
    )(c, w, b.reshape(n_layers, 1, m))


def _tail(x1, gf_ref, sh2, sc2, wr_ref, br_ref, x1_ref, h2_ref, lg_ref):
    h2 = _norm_mod(x1, gf_ref[...], sh2, sc2)
    x1_ref[...] = x1
    h2_ref[...] = h2.astype(BF16)
    lg_ref[...] = jnp.dot(h2, wr_ref[...], preferred_element_type=F32, precision=HIGHEST) + br_ref[...]


def _pool_kernel(*refs, tt, pos0, has_prev):
    if has_prev:
        x_ref, y_ref, gp_ref = refs[:3]
        refs = refs[3:]
    else:
        x_ref = refs[0]
        refs = refs[1:]
    (hist_ref, mod_ref, gm_ref, gf_ref, wp_ref, ps_ref, wr_ref, br_ref,
     x1_ref, h2_ref, lg_ref, hout_ref, ext) = refs
    t = pl.program_id(1)
    x = x_ref[...]
    if has_prev:
        x = x + gp_ref[0] * y_ref[...]
    h = _norm_mod(x, gm_ref[...], mod_ref[0], mod_ref[1])

    @pl.when(t == 0)
    def _():
        ext[0:HIST_ROWS, :] = hist_ref[...]

    ext[HIST_ROWS:HIST_ROWS + tt, :] = h
    pos = pos0 + t * tt + lax.broadcasted_iota(jnp.int32, (tt, 1), 0)
    ys = []
    for g, w in enumerate(POOL_WINDOWS):
        lo, hi = g * POOL_GROUP_DIM, (g + 1) * POOL_GROUP_DIM
        hg = h[:, lo:hi]
        win = hg
        for j in range(1, w):
            win = win + ext[HIST_ROWS - j:HIST_ROWS - j + tt, lo:hi]
        inv_cnt = 1.0 / jnp.minimum(pos + 1, w).astype(F32)
        u = win * inv_cnt - hg
        ys.append(jnp.dot(u.astype(BF16), wp_ref[g], preferred_element_type=F32))
    y = jnp.concatenate(ys, axis=-1) * ps_ref[...]
    x1 = x + mod_ref[2] * y
    last = ext[tt:tt + HIST_ROWS, :]
    ext[0:HIST_ROWS, :] = last
    hout_ref[...] = last
    _tail(x1, gf_ref, mod_ref[3], mod_ref[4], wr_ref, br_ref, x1_ref, h2_ref, lg_ref)


def _pool_call(x, prev, hist16, mods, gm, gf, wp, ps, wr, br, pos0):
    bsz, seq, _ = x.shape
    tt = min(seq, TOKEN_TILE)
    nt = seq // tt
    row = lambda b, t: (b, t, 0)
    const2 = lambda b, t: (0, 0)
    xspec = pl.BlockSpec((None, tt, D_MODEL), row)
    ins, specs = [x], [xspec]
    if prev is not None:
        ins += [prev[0], prev[1]]
        specs += [xspec, pl.BlockSpec((None, 1, 1, D_MODEL), lambda b, t: (b, 0, 0, 0))]
    ins += [hist16, mods, gm, gf, wp, ps, wr, br]
    specs += [pl.BlockSpec((None, HIST_ROWS, D_MODEL), lambda b, t: (b, 0, 0)),
              pl.BlockSpec((None, mods.shape[1], 1, D_MODEL), lambda b, t: (b, 0, 0, 0)),
              pl.BlockSpec((1, D_MODEL), const2), pl.BlockSpec((1, D_MODEL), const2),
              pl.BlockSpec(wp.shape, lambda b, t: (0, 0, 0)),
              pl.BlockSpec((1, D_MODEL), const2),
              pl.BlockSpec((D_MODEL, LANES), const2), pl.BlockSpec((1, LANES), const2)]
    return pl.pallas_call(
        functools.partial(_pool_kernel, tt=tt, pos0=pos0, has_prev=prev is not None),
        grid=(bsz, nt),
        in_specs=specs,
        out_specs=[xspec, xspec, pl.BlockSpec((None, tt, LANES), row),
                   pl.BlockSpec((None, HIST_ROWS, D_MODEL), lambda b, t: (b, 0, 0))],
        out_shape=[jax.ShapeDtypeStruct((bsz, seq, D_MODEL), F32),
                   jax.ShapeDtypeStruct((bsz, seq, D_MODEL), BF16),
                   jax.ShapeDtypeStruct((bsz, seq, LANES), F32),
                   jax.ShapeDtypeStruct((bsz, HIST_ROWS, D_MODEL), F32)],
        scratch_shapes=[pltpu.VMEM((HIST_ROWS + tt, D_MODEL), F32)],
        compiler_params=_params(2),
        name="pool_layer",
    )(*ins)


def _tok_specs(n_tok, tt, mods, tiles_per_mod):
    nmod, rows = mods.shape[1], mods.shape[2]
    xspec = pl.BlockSpec((tt, D_MODEL), lambda i: (i, 0))
    mspec = pl.BlockSpec((None, nmod, rows, D_MODEL), lambda i: (i // tiles_per_mod, 0, 0, 0))
    return xspec, mspec


def _resid_in(refs, has_prev):
    if has_prev:
        x_ref, y_ref, gp_ref = refs[:3]
        return x_ref[...] + gp_ref[0] * y_ref[...], refs[3:]
    return refs[0][...], refs[1:]


def _prev_args(x, prev, xspec, tiles_per_mod):
    ins, specs = [x], [xspec]
    if prev is not None:
        rows = prev[1].shape[2]
        ins += [prev[0], prev[1]]
        specs += [xspec, pl.BlockSpec((None, 1, rows, D_MODEL), lambda i: (i // tiles_per_mod, 0, 0, 0))]
    return ins, specs


def _kv_kernel(*refs, has_prev):
    x, refs = _resid_in(refs, has_prev)
    mod_ref, g_ref, wk_ref, wv_ref, wf_ref, bf_ref, x_out, k_out, v_out, lf_out = refs
    x_out[...] = x
    h = _norm_mod(x, g_ref[...], mod_ref[0], mod_ref[1])
    hb = h.astype(BF16)
    k_out[...] = jnp.dot(hb, wk_ref[...], preferred_element_type=F32)
    v_out[...] = jnp.dot(hb, wv_ref[...], preferred_element_type=F32)
    z = jnp.dot(h, wf_ref[...], preferred_element_type=F32, precision=HIGHEST) + bf_ref[...]
    lf_out[...] = jax.nn.log_sigmoid(z)


def _kv_call(x, prev, mods, g, wk, wv, wf, bf, tiles_per_mod):
    n_tok = x.shape[0]
    tt = min(n_tok, TOKEN_TILE)
    xspec, mspec = _tok_specs(n_tok, tt, mods, tiles_per_mod)
    ins, specs = _prev_args(x, prev, xspec, tiles_per_mod)
    const = lambda i: (0, 0)
    ins += [mods, g, wk, wv, wf, bf]
    specs += [mspec, pl.BlockSpec((1, D_MODEL), const),
              pl.BlockSpec((D_MODEL, KV_WIDTH), const), pl.BlockSpec((D_MODEL, KV_WIDTH), const),
              pl.BlockSpec((D_MODEL, LANES), const), pl.BlockSpec((1, LANES), const)]
    kvspec = pl.BlockSpec((tt, KV_WIDTH), lambda i: (i, 0))
    return pl.pallas_call(
        functools.partial(_kv_kernel, has_prev=prev is not None),
        grid=(n_tok // tt,),
        in_specs=specs,
        out_specs=[xspec, kvspec, kvspec, pl.BlockSpec((tt, LANES), lambda i: (i, 0))],
        out_shape=[jax.ShapeDtypeStruct((n_tok, D_MODEL), F32),
                   jax.ShapeDtypeStruct((n_tok, KV_WIDTH), F32),
                   jax.ShapeDtypeStruct((n_tok, KV_WIDTH), F32),
                   jax.ShapeDtypeStruct((n_tok, LANES), F32)],
        compiler_params=_params(1),
        name="kv_proj",
    )(*ins)


def _bpre_kernel(*refs, has_prev):
    x, refs = _resid_in(refs, has_prev)
    mod_ref, g_ref, wq_ref, q_out = refs
    h = _norm_mod(x, g_ref[...], mod_ref[0], mod_ref[1])
    q = jnp.dot(h.astype(BF16), wq_ref[...], preferred_element_type=F32)
    q_out[...] = (q * (HEAD_DIM ** -0.5)).astype(BF16)


def _bpre_call(x, prev, mods, g, wq, tiles_per_mod):
    n_tok = x.shape[0]
    tt = min(n_tok, TOKEN_TILE)
    xspec, mspec = _tok_specs(n_tok, tt, mods, tiles_per_mod)
    ins, specs = _prev_args(x, prev, xspec, tiles_per_mod)
    const = lambda i: (0, 0)
    ins += [mods, g, wq]
    specs += [mspec, pl.BlockSpec((1, D_MODEL), const), pl.BlockSpec((D_MODEL, D_MODEL), const)]
    return pl.pallas_call(
        functools.partial(_bpre_kernel, has_prev=prev is not None),
        grid=(n_tok // tt,),
        in_specs=specs,
        out_specs=xspec,
        out_shape=jax.ShapeDtypeStruct((n_tok, D_MODEL), BF16),
        compiler_params=_params(1),
        name="q_proj",
    )(*ins)


def _bpost_kernel(*refs, has_prev):
    x, refs = _resid_in(refs, has_prev)
    o_ref, mod_ref, wo_ref, gf_ref, wr_ref, br_ref, x1_ref, h2_ref, lg_ref = refs
    y = jnp.dot(o_ref[...], wo_ref[...], preferred_element_type=F32)
    x1 = x + mod_ref[0] * y
    _tail(x1, gf_ref, mod_ref[1], mod_ref[2], wr_ref, br_ref, x1_ref, h2_ref, lg_ref)


def _bpost_call(x, prev, o, mods, wo, gf, wr, br, tiles_per_mod):
    n_tok = x.shape[0]
    tt = min(n_tok, TOKEN_TILE)
    xspec, mspec = _tok_specs(n_tok, tt, mods, tiles_per_mod)
    ins, specs = _prev_args(x, prev, xspec, tiles_per_mod)
    const = lambda i: (0, 0)
    ins += [o, mods, wo, gf, wr, br]
    specs += [xspec, mspec, pl.BlockSpec((D_MODEL, D_MODEL), const), pl.BlockSpec((1, D_MODEL), const),
              pl.BlockSpec((D_MODEL, LANES), const), pl.BlockSpec((1, LANES), const)]
    return pl.pallas_call(
        functools.partial(_bpost_kernel, has_prev=prev is not None),
        grid=(n_tok // tt,),
        in_specs=specs,
        out_specs=[xspec, xspec, pl.BlockSpec((tt, LANES), lambda i: (i, 0))],
        out_shape=[jax.ShapeDtypeStruct((n_tok, D_MODEL), F32),
                   jax.ShapeDtypeStruct((n_tok, D_MODEL), BF16),
                   jax.ShapeDtypeStruct((n_tok, LANES), F32)],
        compiler_params=_params(1),
        name="o_proj_tail",
    )(*ins)


def _final_kernel(*refs):
    x, refs = _resid_in(refs, True)
    mod_ref, g_ref, y_out = refs
    y_out[...] = _norm_mod(x, g_ref[...], mod_ref[0], mod_ref[1])


def _final_call(x, prev, mods, g, tiles_per_mod):
    n_tok = x.shape[0]
    tt = min(n_tok, TOKEN_TILE)
    xspec, mspec = _tok_specs(n_tok, tt, mods, tiles_per_mod)
    ins, specs = _prev_args(x, prev, xspec, tiles_per_mod)
    ins += [mods, g]
    specs += [mspec, pl.BlockSpec((1, D_MODEL), lambda i: (0, 0))]
    return pl.pallas_call(
        _final_kernel,
        grid=(n_tok // tt,),
        in_specs=specs,
        out_specs=xspec,
        out_shape=jax.ShapeDtypeStruct((n_tok, D_MODEL), F32),
        compiler_params=_params(1),
        name="final_norm",
    )(*ins)


def _route_kernel(lg_ref, idx_ref, gate_ref, rank_ref, cnt_ref, carry, *, tr):
    i = pl.program_id(0)

    @pl.when(i == 0)
    def _():
        carry[...] = jnp.zeros_like(carry)

    lane = lax.broadcasted_iota(jnp.int32, (tr, LANES), 1)
    lanef = lane.astype(F32)
    logit = jnp.where(lane < N_EXPERTS, lg_ref[...], -jnp.inf)
    r_i = lax.broadcasted_iota(jnp.int32, (tr, tr), 0)
    c_i = lax.broadcasted_iota(jnp.int32, (tr, tr), 1)
    earlier = jnp.where(c_i < r_i, 1.0, 0.0).astype(BF16)
    base = carry[...]
    idx_out = jnp.zeros((tr, LANES), F32)
    rank_out = jnp.zeros((tr, LANES), F32)
    val_out = jnp.zeros((tr, LANES), F32)
    top0 = None
    for k in range(TOP_K):
        m = jnp.max(logit, axis=1, keepdims=True)
        sel = jnp.min(jnp.where(logit == m, lanef, float(LANES)), axis=1, keepdims=True)
        onehot = lanef == sel
        ohf = jnp.where(onehot, 1.0, 0.0)
        before = jnp.dot(earlier, ohf.astype(BF16), preferred_element_type=F32)
        rank = jnp.sum(jnp.where(onehot, before + base, 0.0), axis=1, keepdims=True)
        base = base + jnp.sum(ohf, axis=0, keepdims=True)
        logit = jnp.where(onehot, -jnp.inf, logit)
        if k == 0:
            top0 = m
        idx_out = jnp.where(lane == k, sel, idx_out)
        rank_out = jnp.where(lane == k, rank, rank_out)
        val_out = jnp.where(lane == k, jnp.exp(m - top0), val_out)
    carry[...] = base
    cnt_ref[...] = base
    idx_ref[...] = idx_out
    rank_ref[...] = rank_out
    gate_ref[...] = val_out / jnp.sum(val_out, axis=1, keepdims=True)


def _route_call(logits):
    n_tok = logits.shape[0]
    tr = min(n_tok, TOKEN_TILE)
    spec = pl.BlockSpec((tr, LANES), lambda i: (i, 0))
    sds = jax.ShapeDtypeStruct((n_tok, LANES), F32)
    return pl.pallas_call(
        functools.partial(_route_kernel, tr=tr),
        grid=(n_tok // tr,),
        in_specs=[spec],
        out_specs=[spec, spec, spec, pl.BlockSpec((1, LANES), lambda i: (0, 0))],
        out_shape=[sds, sds, sds, jax.ShapeDtypeStruct((1, LANES), F32)],
        scratch_shapes=[pltpu.VMEM((1, LANES), F32)],
        compiler_params=_params(1),
        name="route",
    )(logits)


def _moe_kernel(be_ref, nv_ref, x_ref, wgu_ref, bgu_ref, wd_ref, bd_ref, o_ref):
    i = pl.program_id(0)

    @pl.when(i < nv_ref[0])
    def _():
        x = x_ref[...]
        acc = None
        for c in range(D_FF // FF_CHUNK):
            lo = c * FF_CHUNK
            glu = jnp.dot(x, wgu_ref[:, lo:lo + FF_CHUNK], preferred_element_type=F32) + bgu_ref[:, lo:lo + FF_CHUNK]
            lin = (jnp.dot(x, wgu_ref[:, D_FF + lo:D_FF + lo + FF_CHUNK], preferred_element_type=F32)
                   + bgu_ref[:, D_FF + lo:D_FF + lo + FF_CHUNK])
            glu = jnp.minimum(glu, SWIGLU_LIMIT)
            lin = jnp.clip(lin, -SWIGLU_LIMIT, SWIGLU_LIMIT)
            act = glu * jax.nn.sigmoid(SWIGLU_ALPHA * glu) * (lin + 1.0)
            part = jnp.dot(act.astype(BF16), wd_ref[lo:lo + FF_CHUNK, :], preferred_element_type=F32)
            acc = part if acc is None else acc + part
        o_ref[...] = acc + bd_ref[...]

    @pl.when(i >= nv_ref[0])
    def _():
        o_ref[...] = jnp.zeros_like(o_ref)


def _moe_call(xs, blk_expert, n_valid, wgu, bgu, wd, bd, tm):
    n_rows = xs.shape[0]
    grid_spec = pltpu.PrefetchScalarGridSpec(
        num_scalar_prefetch=2,
        grid=(n_rows // tm,),
        in_specs=[pl.BlockSpec((tm, D_MODEL), lambda i, be, nv: (i, 0)),
                  pl.BlockSpec((None, D_MODEL, 2 * D_FF), lambda i, be, nv: (be[i], 0, 0)),
                  pl.BlockSpec((None, 1, 2 * D_FF), lambda i, be, nv: (be[i], 0, 0)),
                  pl.BlockSpec((None, D_FF, D_MODEL), lambda i, be, nv: (be[i], 0, 0)),
                  pl.BlockSpec((None, 1, D_MODEL), lambda i, be, nv: (be[i], 0, 0))],
        out_specs=pl.BlockSpec((tm, D_MODEL), lambda i, be, nv: (i, 0)),
    )
    return pl.pallas_call(
        _moe_kernel,
        grid_spec=grid_spec,
        out_shape=jax.ShapeDtypeStruct((n_rows, D_MODEL), F32),
        compiler_params=_params(1),
        name="moe_experts",
    )(blk_expert, n_valid, xs, wgu, bgu, wd, bd)


def _moe(h2, logits, lw, tm):
    n_tok = h2.shape[0]
    idx_f, gate_f, rank_f, cnt_f = _route_call(logits)
    idx = idx_f[:, :TOP_K].astype(jnp.int32)
    gate = gate_f[:, :TOP_K]
    rank = rank_f[:, :TOP_K].astype(jnp.int32)
    counts = cnt_f[0, :N_EXPERTS].astype(jnp.int32)
    padded = (counts + tm - 1) // tm * tm
    pad_end = jnp.cumsum(padded)
    pad_start = pad_end - padded
    pos = pad_start[idx] + rank
    n_blocks = n_tok * TOP_K // tm + N_EXPERTS
    n_rows = n_blocks * tm
    tok = jnp.broadcast_to(jnp.arange(n_tok, dtype=jnp.int32)[:, None], (n_tok, TOP_K))
    row_tok = jnp.zeros((n_rows,), jnp.int32).at[pos.reshape(-1)].set(tok.reshape(-1))
    blk_start = jnp.arange(n_blocks, dtype=jnp.int32) * tm
    blk_expert = jnp.minimum(jnp.sum(pad_end[None, :] <= blk_start[:, None], axis=1), N_EXPERTS - 1).astype(jnp.int32)
    n_valid = (pad_end[-1] // tm).astype(jnp.int32).reshape(1)
    xs = jnp.take(h2, row_tok, axis=0)
    out = _moe_call(xs, blk_expert, n_valid, lw["wgu"], lw["bgu"], lw["wd"], lw["bd"], tm)
    picked = jnp.take(out, pos.reshape(-1), axis=0).reshape(n_tok, TOP_K, D_MODEL)
    return jnp.sum(picked * gate[:, :, None], axis=1)


def _cumsum_kernel(x_ref, o_ref, carry, *, tc):
    j = pl.program_id(1)

    @pl.when(j == 0)
    def _():
        carry[...] = jnp.zeros_like(carry)

    r_i = lax.broadcasted_iota(jnp.int32, (tc, tc), 0)
    c_i = lax.broadcasted_iota(jnp.int32, (tc, tc), 1)
    upto = jnp.where(r_i <= c_i, 1.0, 0.0)
    cs = jnp.dot(x_ref[...], upto, preferred_element_type=F32, precision=HIGHEST) + carry[...]
    o_ref[...] = cs
    carry[...] = cs[:, tc - 1:tc]


def _cumsum_call(x):
    bsz, nh, tk = x.shape
    tc = LANES
    spec = pl.BlockSpec((None, nh, tc), lambda b, j: (b, 0, j))
    return pl.pallas_call(
        functools.partial(_cumsum_kernel, tc=tc),
        grid=(bsz, tk // tc),
        in_specs=[spec],
        out_specs=spec,
        out_shape=jax.ShapeDtypeStruct(x.shape, F32),
        scratch_shapes=[pltpu.VMEM((nh, 1), F32)],
        compiler_params=_params(2),
        name="logf_cumsum",
    )(x)


def _attn_kernel(q_ref, k_ref, v_ref, cq_ref, ck_ref, o_ref, m_sc, l_sc, acc_sc, *, tq, tk, pos0, nk):
    qi = pl.program_id(2)
    ki = pl.program_id(3)

    @pl.when(ki == 0)
    def _():
        m_sc[...] = jnp.full_like(m_sc, NEG_INF)
        l_sc[...] = jnp.zeros_like(l_sc)
        acc_sc[...] = jnp.zeros_like(acc_sc)

    q_first = pos0 + qi * tq

    @pl.when(ki * tk <= q_first + tq - 1)
    def _():
        k = k_ref[...]
        v = v_ref[...]
        rows = q_first + lax.broadcasted_iota(jnp.int32, (tq, tk), 0)
        cols = ki * tk + lax.broadcasted_iota(jnp.int32, (tq, tk), 1)
        visible = cols <= rows
        for g in range(KV_GROUP):
            qg = q_ref[:, g * HEAD_DIM:(g + 1) * HEAD_DIM]
            s = lax.dot_general(qg, k, (((1,), (1,)), ((), ())), preferred_element_type=F32)
            s = s + cq_ref[:, g:g + 1] - ck_ref[g:g + 1, :]
            s = jnp.where(visible, s, NEG_INF)
            m_prev = m_sc[g]
            m_new = jnp.maximum(m_prev, jnp.max(s, axis=1, keepdims=True))
            alpha = jnp.exp(m_prev - m_new)
            p = jnp.exp(s - m_new)
            l_sc[g] = alpha * l_sc[g] + jnp.sum(p, axis=1, keepdims=True)
            acc_sc[g] = alpha * acc_sc[g] + jnp.dot(p.astype(BF16), v, preferred_element_type=F32)
            m_sc[g] = m_new

    @pl.when(ki == nk - 1)
    def _():
        for g in range(KV_GROUP):
            o_ref[:, g * HEAD_DIM:(g + 1) * HEAD_DIM] = (acc_sc[g] / l_sc[g]).astype(BF16)


def _attn_call(q, k, v, cq, ck, pos0, tq, tk):
    bsz, seq_q, _ = q.shape
    seq_k = k.shape[2]
    nq, nk = seq_q // tq, seq_k // tk
    last_k = lambda qi: (pos0 + qi * tq + tq - 1) // tk
    kv_map = lambda b, h, qi, ki: (b, h, jnp.minimum(ki, last_k(qi)), 0)
    return pl.pallas_call(
        functools.partial(_attn_kernel, tq=tq, tk=tk, pos0=pos0, nk=nk),
        grid=(bsz, N_KV_HEADS, nq, nk),
        in_specs=[pl.BlockSpec((None, tq, KV_GROUP * HEAD_DIM), lambda b, h, qi, ki: (b, qi, h)),
                  pl.BlockSpec((None, None, tk, HEAD_DIM), kv_map),
                  pl.BlockSpec((None, None, tk, HEAD_DIM), kv_map),
                  pl.BlockSpec((None, None, tq, KV_GROUP), lambda b, h, qi, ki: (b, h, qi, 0)),
                  pl.BlockSpec((None, None, KV_GROUP, tk),
                               lambda b, h, qi, ki: (b, h, 0, jnp.minimum(ki, last_k(qi))))],
        out_specs=pl.BlockSpec((None, tq, KV_GROUP * HEAD_DIM), lambda b, h, qi, ki: (b, qi, h)),
        out_shape=jax.ShapeDtypeStruct(q.shape, BF16),
        scratch_shapes=[pltpu.VMEM((KV_GROUP, tq, 1), F32), pltpu.VMEM((KV_GROUP, tq, 1), F32),
                        pltpu.VMEM((KV_GROUP, tq, HEAD_DIM), F32)],
        compiler_params=_params(4),
        name="forget_attention",
    )(q, k, v, cq, ck)


def _split_mods(m, n, expand_t):
    bsz = m.shape[0]
    m = m.reshape(bsz, n, 1, D_MODEL)
    if expand_t:
        m = jnp.broadcast_to(m.transpose(1, 0, 2, 3), (n, bsz, expand_t, D_MODEL)).reshape(1, n, bsz * expand_t, D_MODEL)
    return m


def _trunk(x, ada, ada_kv, ada_final, pool_hist, kv_past, pos0, w):
    bsz, seq, _ = x.shape
    n_tok = bsz * seq
    flat_small = seq < TOKEN_TILE
    expand_t = seq if flat_small else 0
    tiles_per_mod = 1 if flat_small else seq // TOKEN_TILE
    tm = MOE_TILE if n_tok * TOP_K >= 64 * MOE_TILE else 128
    hist16 = jnp.pad(pool_hist, ((0, 0), (0, 0), (1, 0), (0, 0)))
    new_pool = []
    pending = None
    flat_prev = lambda p: None if p is None else (p[0], _split_mods(p[1], 1, expand_t))
    xf = x.reshape(n_tok, D_MODEL)
    k4 = v4 = logf_new = k_att = v_att = cq = ck = None
    for layer in range(DEPTH):
        lw = w["layers"][layer]
        m6 = ada[layer]
        if layer < N_A_LAYERS:
            mods = _split_mods(m6[:, :5 * D_MODEL], 5, 0)
            pprev = None
            if pending is not None:
                pprev = (pending[0].reshape(bsz, seq, D_MODEL), _split_mods(pending[1], 1, 0))
            x1, h2, logits, hout = _pool_call(
                xf.reshape(bsz, seq, D_MODEL), pprev, hist16[layer], mods, lw["gm"], lw["gf"],
                lw["w_pool"], lw["pool_scale"], lw["wr"], lw["br"], pos0)
            new_pool.append(hout[:, 1:])
            x1 = x1.reshape(n_tok, D_MODEL)
            h2 = h2.reshape(n_tok, D_MODEL)
            logits = logits.reshape(n_tok, LANES)
        else:
            mods = _split_mods(m6[:, :5 * D_MODEL], 5, expand_t)
            q = _bpre_call(xf, flat_prev(pending), mods[:, 0:2], lw["gm"], lw["wq"], tiles_per_mod)
            o = _attn_call(q.reshape(bsz, seq, D_MODEL), k_att, v_att, cq, ck, pos0,
                           min(seq, ATTN_TQ), ATTN_TK if seq >= ATTN_TK else k_att.shape[2])
            x1, h2, logits = _bpost_call(xf, flat_prev(pending), o.reshape(n_tok, D_MODEL), mods[:, 2:5], lw["wo"],
                                         lw["gf"], lw["wr"], lw["br"], tiles_per_mod)
        y = _moe(h2, logits, lw, tm)
        xf, pending = x1, (y, m6[:, 5 * D_MODEL:])
        if layer == N_A_LAYERS - 1:
            mods_kv = _split_mods(ada_kv, 2, expand_t)
            xf, k_new, v_new, lf = _kv_call(xf, flat_prev(pending), mods_kv, w["g_kv"], w["wk"], w["wv"], w["wf"],
                                            w["bf"], tiles_per_mod)
            pending = None
            logf_new = lf[:, :N_HEADS].reshape(bsz, seq, N_HEADS)
            k4 = k_new.reshape(bsz, seq, N_KV_HEADS, HEAD_DIM)
            v4 = v_new.reshape(bsz, seq, N_KV_HEADS, HEAD_DIM)
            if kv_past is None:
                k_all, v_all, logf_all = k4, v4, logf_new
            else:
                k_all = jnp.concatenate([kv_past[0], k4], axis=1)
                v_all = jnp.concatenate([kv_past[1], v4], axis=1)
                logf_all = jnp.concatenate([kv_past[2], logf_new], axis=1)
            seq_k = k_all.shape[1]
            seq_kp = -(-seq_k // LANES) * LANES
            padk = ((0, 0), (0, seq_kp - seq_k), (0, 0), (0, 0))
            k_att = jnp.pad(k_all, padk).astype(BF16).transpose(0, 2, 1, 3)
            v_att = jnp.pad(v_all, padk).astype(BF16).transpose(0, 2, 1, 3)
            lf_t = jnp.pad(logf_all, padk[:3]).transpose(0, 2, 1)
            cum = _cumsum_call(lf_t)
            ck = cum.reshape(bsz, N_KV_HEADS, KV_GROUP, seq_kp)
            cq = cum[:, :, seq_k - seq:seq_k].reshape(bsz, N_KV_HEADS, KV_GROUP, seq).transpose(0, 1, 3, 2)
    mods_f = _split_mods(ada_final, 2, expand_t)
    y_out = _final_call(xf, flat_prev(pending), mods_f, w["g_final"], tiles_per_mod)
    return (y_out.reshape(bsz, seq, D_MODEL), k4, v4, logf_new, jnp.stack(new_pool, axis=0))


def _pad_lanes(a, axis):
    pad = [(0, 0)] * a.ndim
    pad[axis] = (0, LANES - a.shape[axis])
    return jnp.pad(a, pad)


def kernel(x_prompt, x_sample, cache_k, cache_v, cache_logf, state_pool, c_prompt, c_sample, norm_mix_g, norm_ffn_g, w_ada, b_ada, w_pool, pool_scale, norm_kv_g, w_ada_kv, b_ada_kv, w_kv, b_forget, w_q, w_o, w_router, b_router, w_gate_up, b_gate_up, w_down, b_down, norm_final_g, w_ada_final, b_ada_final):
    bp, bs = c_prompt.shape[0], c_sample.shape[0]
    c_all = jnp.concatenate([c_prompt, c_sample], axis=0)
    ada = _ada_call(c_all, w_ada, b_ada)
    w_small = jnp.stack([w_ada_kv, w_ada_final])
    b_small = jnp.stack([b_ada_kv, b_ada_final])
    ada_small = _ada_call(c_all, w_small, b_small)

    layers = []
    for layer in range(DEPTH):
        lw = dict(gm=norm_mix_g[layer][None], gf=norm_ffn_g[layer][None],
                  wr=_pad_lanes(w_router[layer], 1), br=_pad_lanes(b_router[layer][None], 1),
                  wgu=w_gate_up[layer].astype(BF16), bgu=b_gate_up[layer][:, None, :],
                  wd=w_down[layer].astype(BF16), bd=b_down[layer][:, None, :])
        if layer < N_A_LAYERS:
            lw.update(w_pool=w_pool[layer].astype(BF16), pool_scale=pool_scale[layer][None])
        else:
            j = layer - N_A_LAYERS
            lw.update(wq=w_q[j].astype(BF16), wo=w_o[j].astype(BF16))
        layers.append(lw)
    w = dict(layers=layers, g_kv=norm_kv_g[None], g_final=norm_final_g[None],
             wk=w_kv[:, :KV_WIDTH].astype(BF16), wv=w_kv[:, KV_WIDTH:2 * KV_WIDTH].astype(BF16),
             wf=_pad_lanes(w_kv[:, 2 * KV_WIDTH:], 1), bf=_pad_lanes(b_forget[None], 1))

    zero_hist = jnp.zeros((N_A_LAYERS, bp, POOL_HIST, D_MODEL), F32)
    y_p, k_p, v_p, lf_p, pool_p = _trunk(x_prompt, ada[:, :bp], ada_small[0, :bp], ada_small[1, :bp],
                                         zero_hist, None, 0, w)
    y_s, k_s, v_s, lf_s, pool_s = _trunk(x_sample, ada[:, bp:], ada_small[0, bp:], ada_small[1, bp:],
                                         state_pool, (cache_k, cache_v, cache_logf), cache_k.shape[1], w)
    return (y_p, y_s, k_p, v_p, lf_p, pool_p, k_s, v_s, lf_s, pool_s)
```

```python
import functools

import jax
import jax.numpy as jnp
from jax import lax
from jax.experimental import pallas as pl
from jax.experimental.pallas import tpu as pltpu

F32 = jnp.float32
BF16 = jnp.bfloat16
HIGHEST = lax.Precision.HIGHEST

D_MODEL = 1024
DEPTH = 4
N_A_LAYERS = DEPTH // 2
POOL_WINDOWS = (2, 4, 8, 16)
POOL_GROUP_DIM = D_MODEL // len(POOL_WINDOWS)
POOL_HIST = max(POOL_WINDOWS) - 1
HIST_ROWS = POOL_HIST + 1
N_HEADS = 16
HEAD_DIM = D_MODEL // N_HEADS
N_KV_HEADS = 4
KV_GROUP = N_HEADS // N_KV_HEADS
KV_WIDTH = N_KV_HEADS * HEAD_DIM
N_EXPERTS = 32
TOP_K = 4
D_FF = D_MODEL
SWIGLU_LIMIT = 7.0
SWIGLU_ALPHA = 1.702
RMS_EPS = 1e-6
NEG_INF = -1e30

LANES = 128
TOKEN_TILE = 512
MOE_TILE = 256
FF_CHUNK = 512
ATTN_TQ = 256
ATTN_TK = 256
VMEM_LIMIT = 56 * 1024 * 1024


def _params(n_grid_dims):
    return pltpu.CompilerParams(dimension_semantics=("arbitrary",) * n_grid_dims, vmem_limit_bytes=VMEM_LIMIT)


def _norm_mod(x, gain, shift, scale):
    y = x * lax.rsqrt(jnp.mean(x * x, axis=-1, keepdims=True) + RMS_EPS) * gain
    return y * (1.0 + scale) + shift


def _ada_kernel(c_ref, w_ref, b_ref, o_ref):
    c = c_ref[...]
    s = c * jax.nn.sigmoid(c)
    o_ref[...] = jnp.dot(s, w_ref[...], preferred_element_type=F32, precision=HIGHEST) + b_ref[...]


def _ada_call(c, w, b):
    n_layers, _, m = w.shape
    bc = c.shape[0]
    tn = 1024
    return pl.pallas_call(
        _ada_kernel,
        grid=(n_layers, m // tn),
        in_specs=[pl.BlockSpec((bc, D_MODEL), lambda l, j: (0, 0)),
                  pl.BlockSpec((None, D_MODEL, tn), lambda l, j: (l, 0, j)),
                  pl.BlockSpec((None, 1, tn), lambda l, j: (l, 0, j))],
        out_specs=pl.BlockSpec((None, bc, tn), lambda l, j: (l, 0, j)),
        out_shape=jax.ShapeDtypeStruct((n_layers, bc, m), F32),
        compiler_params=_params(2),
        name="ada",
    )(c, w, b.reshape(n_layers, 1, m))


def _tail(x1, gf_ref, sh2, sc2, wr_ref, br_ref, x1_ref, h2_ref, lg_ref):
    h2 = _norm_mod(x1, gf_ref[...], sh2, sc2)
    x1_ref[...] = x1
    h2_ref[...] = h2.astype(BF16)
    lg_ref[...] = jnp.dot(h2, wr_ref[...], preferred_element_type=F32, precision=HIGHEST) + br_ref[...]


def _pool_kernel(*refs, tt, pos0, has_prev):
    if has_prev:
        x_ref, y_ref, gp_ref = refs[:3]
        refs = refs[3:]
    else:
        x_ref = refs[0]
        refs = refs[1:]
    (hist_ref, mod_ref, gm_ref, gf_ref, wp_ref, ps_ref, wr_ref, br_ref,
     x1_ref, h2_ref, lg_ref, hout_ref, ext) = refs
    t = pl.program_id(1)
    x = x_ref[...]
    if has_prev:
        x = x + gp_ref[0] * y_ref[...]
    h = _norm_mod(x, gm_ref[...], mod_ref[0], mod_ref[1])

    @pl.when(t == 0)
    def _():
        ext[0:HIST_ROWS, :] = hist_ref[...]

    ext[HIST_ROWS:HIST_ROWS + tt, :] = h
    pos = pos0 + t * tt + lax.broadcasted_iota(jnp.int32, (tt, 1), 0)
    ys = []
    for g, w in enumerate(POOL_WINDOWS):
        lo, hi = g * POOL_GROUP_DIM, (g + 1) * POOL_GROUP_DIM
        hg = h[:, lo:hi]
        win = hg
        for j in range(1, w):
            win = win + ext[HIST_ROWS - j:HIST_ROWS - j + tt, lo:hi]
        inv_cnt = 1.0 / jnp.minimum(pos + 1, w).astype(F32)
        u = win * inv_cnt - hg
        ys.append(jnp.dot(u.astype(BF16), wp_ref[g], preferred_element_type=F32))
    y = jnp.concatenate(ys, axis=-1) * ps_ref[...]
    x1 = x + mod_ref[2] * y
    last = ext[tt:tt + HIST_ROWS, :]
    ext[0:HIST_ROWS, :] = last
    hout_ref[...] = last
    _tail(x1, gf_ref, mod_ref[3], mod_ref[4], wr_ref, br_ref, x1_ref, h2_ref, lg_ref)


def _pool_call(x, prev, hist16, mods, gm, gf, wp, ps, wr, br, pos0):
    bsz, seq, _ = x.shape
    tt = min(seq, TOKEN_TILE)
    nt = seq // tt
    row = lambda b, t: (b, t, 0)
    const2 = lambda b, t: (0, 0)
    xspec = pl.BlockSpec((None, tt, D_MODEL), row)
    ins, specs = [x], [xspec]
    if prev is not None:
        ins += [prev[0], prev[1]]
        specs += [xspec, pl.BlockSpec((None, 1, 1, D_MODEL), lambda b, t: (b, 0, 0, 0))]
    ins += [hist16, mods, gm, gf, wp, ps, wr, br]
    specs += [pl.BlockSpec((None, HIST_ROWS, D_MODEL), lambda b, t: (b, 0, 0)),
              pl.BlockSpec((None, mods.shape[1], 1, D_MODEL), lambda b, t: (b, 0, 0, 0)),
              pl.BlockSpec((1, D_MODEL), const2), pl.BlockSpec((1, D_MODEL), const2),
              pl.BlockSpec(wp.shape, lambda b, t: (0, 0, 0)),
              pl.BlockSpec((1, D_MODEL), const2),
              pl.BlockSpec((D_MODEL, LANES), const2), pl.BlockSpec((1, LANES), const2)]
    return pl.pallas_call(
        functools.partial(_pool_kernel, tt=tt, pos0=pos0, has_prev=prev is not None),
        grid=(bsz, nt),
        in_specs=specs,
        out_specs=[xspec, xspec, pl.BlockSpec((None, tt, LANES), row),
                   pl.BlockSpec((None, HIST_ROWS, D_MODEL), lambda b, t: (b, 0, 0))],
        out_shape=[jax.ShapeDtypeStruct((bsz, seq, D_MODEL), F32),
                   jax.ShapeDtypeStruct((bsz, seq, D_MODEL), BF16),
                   jax.ShapeDtypeStruct((bsz, seq, LANES), F32),
                   jax.ShapeDtypeStruct((bsz, HIST_ROWS, D_MODEL), F32)],
        scratch_shapes=[pltpu.VMEM((HIST_ROWS + tt, D_MODEL), F32)],
        compiler_params=_params(2),
        name="pool_layer",
    )(*ins)


def _tok_specs(n_tok, tt, mods, tiles_per_mod):
    nmod, rows = mods.shape[1], mods.shape[2]
    xspec = pl.BlockSpec((tt, D_MODEL), lambda i: (i, 0))
    mspec = pl.BlockSpec((None, nmod, rows, D_MODEL), lambda i: (i // tiles_per_mod, 0, 0, 0))
    return xspec, mspec


def _resid_in(refs, has_prev):
    if has_prev:
        x_ref, y_ref, gp_ref = refs[:3]
        return x_ref[...] + gp_ref[0] * y_ref[...], refs[3:]
    return refs[0][...], refs[1:]


def _prev_args(x, prev, xspec, tiles_per_mod):
    ins, specs = [x], [xspec]
    if prev is not None:
        rows = prev[1].shape[2]
        ins += [prev[0], prev[1]]
        specs += [xspec, pl.BlockSpec((None, 1, rows, D_MODEL), lambda i: (i // tiles_per_mod, 0, 0, 0))]
    return ins, specs


def _kv_kernel(*refs, has_prev):
    x, refs = _resid_in(refs, has_prev)
    mod_ref, g_ref, wk_ref, wv_ref, wf_ref, bf_ref, x_out, k_out, v_out, lf_out = refs
    x_out[...] = x
    h = _norm_mod(x, g_ref[...], mod_ref[0], mod_ref[1])
    hb = h.astype(BF16)
    k_out[...] = jnp.dot(hb, wk_ref[...], preferred_element_type=F32)
    v_out[...] = jnp.dot(hb, wv_ref[...], preferred_element_type=F32)
    z = jnp.dot(h, wf_ref[...], preferred_element_type=F32, precision=HIGHEST) + bf_ref[...]
    lf_out[...] = jax.nn.log_sigmoid(z)


def _kv_call(x, prev, mods, g, wk, wv, wf, bf, tiles_per_mod):
    n_tok = x.shape[0]
    tt = min(n_tok, TOKEN_TILE)
    xspec, mspec = _tok_specs(n_tok, tt, mods, tiles_per_mod)
    ins, specs = _prev_args(x, prev, xspec, tiles_per_mod)
    const = lambda i: (0, 0)
    ins += [mods, g, wk, wv, wf, bf]
    specs += [mspec, pl.BlockSpec((1, D_MODEL), const),
              pl.BlockSpec((D_MODEL, KV_WIDTH), const), pl.BlockSpec((D_MODEL, KV_WIDTH), const),
              pl.BlockSpec((D_MODEL, LANES), const), pl.BlockSpec((1, LANES), const)]
    kvspec = pl.BlockSpec((tt, KV_WIDTH), lambda i: (i, 0))
    return pl.pallas_call(
        functools.partial(_kv_kernel, has_prev=prev is not None),
        grid=(n_tok // tt,),
        in_specs=specs,
        out_specs=[xspec, kvspec, kvspec, pl.BlockSpec((tt, LANES), lambda i: (i, 0))],
        out_shape=[jax.ShapeDtypeStruct((n_tok, D_MODEL), F32),
                   jax.ShapeDtypeStruct((n_tok, KV_WIDTH), F32),
                   jax.ShapeDtypeStruct((n_tok, KV_WIDTH), F32),
                   jax.ShapeDtypeStruct((n_tok, LANES), F32)],
        compiler_params=_params(1),
        name="kv_proj",
    )(*ins)


def _bpre_kernel(*refs, has_prev):
    x, refs = _resid_in(refs, has_prev)
    mod_ref, g_ref, wq_ref, q_out = refs
    h = _norm_mod(x, g_ref[...], mod_ref[0], mod_ref[1])
    q = jnp.dot(h.astype(BF16), wq_ref[...], preferred_element_type=F32)
    q_out[...] = (q * (HEAD_DIM ** -0.5)).astype(BF16)


def _bpre_call(x, prev, mods, g, wq, tiles_per_mod):
    n_tok = x.shape[0]
    tt = min(n_tok, TOKEN_TILE)
    xspec, mspec = _tok_specs(n_tok, tt, mods, tiles_per_mod)
    ins, specs = _prev_args(x, prev, xspec, tiles_per_mod)
    const = lambda i: (0, 0)
    ins += [mods, g, wq]
    specs += [mspec, pl.BlockSpec((1, D_MODEL), const), pl.BlockSpec((D_MODEL, D_MODEL), const)]
    return pl.pallas_call(
        functools.partial(_bpre_kernel, has_prev=prev is not None),
        grid=(n_tok // tt,),
        in_specs=specs,
        out_specs=xspec,
        out_shape=jax.ShapeDtypeStruct((n_tok, D_MODEL), BF16),
        compiler_params=_params(1),
        name="q_proj",
    )(*ins)


def _bpost_kernel(*refs, has_prev):
    x, refs = _resid_in(refs, has_prev)
    o_ref, mod_ref, wo_ref, gf_ref, wr_ref, br_ref, x1_ref, h2_ref, lg_ref = refs
    y = jnp.dot(o_ref[...], wo_ref[...], preferred_element_type=F32)
    x1 = x + mod_ref[0] * y
    _tail(x1, gf_ref, mod_ref[1], mod_ref[2], wr_ref, br_ref, x1_ref, h2_ref, lg_ref)


def _bpost_call(x, prev, o, mods, wo, gf, wr, br, tiles_per_mod):
    n_tok = x.shape[0]
    tt = min(n_tok, TOKEN_TILE)
    xspec, mspec = _tok_specs(n_tok, tt, mods, tiles_per_mod)
    ins, specs = _prev_args(x, prev, xspec, tiles_per_mod)
    const = lambda i: (0, 0)
    ins += [o, mods, wo, gf, wr, br]
    specs += [xspec, mspec, pl.BlockSpec((D_MODEL, D_MODEL), const), pl.BlockSpec((1, D_MODEL), const),
              pl.BlockSpec((D_MODEL, LANES), const), pl.BlockSpec((1, LANES), const)]
    return pl.pallas_call(
        functools.partial(_bpost_kernel, has_prev=prev is not None),
        grid=(n_tok // tt,),
        in_specs=specs,
        out_specs=[xspec, xspec, pl.BlockSpec((tt, LANES), lambda i: (i, 0))],
        out_shape=[jax.ShapeDtypeStruct((n_tok, D_MODEL), F32),
                   jax.ShapeDtypeStruct((n_tok, D_MODEL), BF16),
                   jax.ShapeDtypeStruct((n_tok, LANES), F32)],
        compiler_params=_params(1),
        name="o_proj_tail",
    )(*ins)


def _final_kernel(*refs):
    x, refs = _resid_in(refs, True)
    mod_ref, g_ref, y_out = refs
    y_out[...] = _norm_mod(x, g_ref[...], mod_ref[0], mod_ref[1])


def _final_call(x, prev, mods, g, tiles_per_mod):
    n_tok = x.shape[0]
    tt = min(n_tok, TOKEN_TILE)
    xspec, mspec = _tok_specs(n_tok, tt, mods, tiles_per_mod)
    ins, specs = _prev_args(x, prev, xspec, tiles_per_mod)
    ins += [mods, g]
    specs += [mspec, pl.BlockSpec((1, D_MODEL), lambda i: (0, 0))]
    return pl.pallas_call(
        _final_kernel,
        grid=(n_tok // tt,),
        in_specs=specs,
        out_specs=xspec,
        out_shape=jax.ShapeDtypeStruct((n_tok, D_MODEL), F32),
        compiler_params=_params(1),
        name="final_norm",
    )(*ins)


def _route_kernel(lg_ref, idx_ref, gate_ref, rank_ref, cnt_ref, carry, *, tr):
    i = pl.program_id(0)

    @pl.when(i == 0)
    def _():
        carry[...] = jnp.zeros_like(carry)

    lane = lax.broadcasted_iota(jnp.int32, (tr, LANES), 1)
    lanef = lane.astype(F32)
    logit = jnp.where(lane < N_EXPERTS, lg_ref[...], -jnp.inf)
    r_i = lax.broadcasted_iota(jnp.int32, (tr, tr), 0)
    c_i = lax.broadcasted_iota(jnp.int32, (tr, tr), 1)
    earlier = jnp.where(c_i < r_i, 1.0, 0.0).astype(BF16)
    base = carry[...]
    idx_out = jnp.zeros((tr, LANES), F32)
    rank_out = jnp.zeros((tr, LANES), F32)
    val_out = jnp.zeros((tr, LANES), F32)
    top0 = None
    for k in range(TOP_K):
        m = jnp.max(logit, axis=1, keepdims=True)
        sel = jnp.min(jnp.where(logit == m, lanef, float(LANES)), axis=1, keepdims=True)
        onehot = lanef == sel
        ohf = jnp.where(onehot, 1.0, 0.0)
        before = jnp.dot(earlier, ohf.astype(BF16), preferred_element_type=F32)
        rank = jnp.sum(jnp.where(onehot, before + base, 0.0), axis=1, keepdims=True)
        base = base + jnp.sum(ohf, axis=0, keepdims=True)
        logit = jnp.where(onehot, -jnp.inf, logit)
        if k == 0:
            top0 = m
        idx_out = jnp.where(lane == k, sel, idx_out)
        rank_out = jnp.where(lane == k, rank, rank_out)
        val_out = jnp.where(lane == k, jnp.exp(m - top0), val_out)
    carry[...] = base
    cnt_ref[...] = base
    idx_ref[...] = idx_out
    rank_ref[...] = rank_out
    gate_ref[...] = val_out / jnp.sum(val_out, axis=1, keepdims=True)


def _route_call(logits):
    n_tok = logits.shape[0]
    tr = min(n_tok, TOKEN_TILE)
    spec = pl.BlockSpec((tr, LANES), lambda i: (i, 0))
    sds = jax.ShapeDtypeStruct((n_tok, LANES), F32)
    return pl.pallas_call(
        functools.partial(_route_kernel, tr=tr),
        grid=(n_tok // tr,),
        in_specs=[spec],
        out_specs=[spec, spec, spec, pl.BlockSpec((1, LANES), lambda i: (0, 0))],
        out_shape=[sds, sds, sds, jax.ShapeDtypeStruct((1, LANES), F32)],
        scratch_shapes=[pltpu.VMEM((1, LANES), F32)],
        compiler_params=_params(1),
        name="route",
    )(logits)


def _moe_kernel(be_ref, nv_ref, x_ref, wgu_ref, bgu_ref, wd_ref, bd_ref, o_ref):
    i = pl.program_id(0)

    @pl.when(i < nv_ref[0])
    def _():
        x = x_ref[...]
        acc = None
        for c in range(D_FF // FF_CHUNK):
            lo = c * FF_CHUNK
            glu = jnp.dot(x, wgu_ref[:, lo:lo + FF_CHUNK], preferred_element_type=F32) + bgu_ref[:, lo:lo + FF_CHUNK]
            lin = (jnp.dot(x, wgu_ref[:, D_FF + lo:D_FF + lo + FF_CHUNK], preferred_element_type=F32)
                   + bgu_ref[:, D_FF + lo:D_FF + lo + FF_CHUNK])
            glu = jnp.minimum(glu, SWIGLU_LIMIT)
            lin = jnp.clip(lin, -SWIGLU_LIMIT, SWIGLU_LIMIT)
            act = glu * jax.nn.sigmoid(SWIGLU_ALPHA * glu) * (lin + 1.0)
            part = jnp.dot(act.astype(BF16), wd_ref[lo:lo + FF_CHUNK, :], preferred_element_type=F32)
            acc = part if acc is None else acc + part
        o_ref[...] = acc + bd_ref[...]

    @pl.when(i >= nv_ref[0])
    def _():
        o_ref[...] = jnp.zeros_like(o_ref)


def _moe_call(xs, blk_expert, n_valid, wgu, bgu, wd, bd, tm):
    n_rows = xs.shape[0]
    grid_spec = pltpu.PrefetchScalarGridSpec(
        num_scalar_prefetch=2,
        grid=(n_rows // tm,),
        in_specs=[pl.BlockSpec((tm, D_MODEL), lambda i, be, nv: (i, 0)),
                  pl.BlockSpec((None, D_MODEL, 2 * D_FF), lambda i, be, nv: (be[i], 0, 0)),
                  pl.BlockSpec((None, 1, 2 * D_FF), lambda i, be, nv: (be[i], 0, 0)),
                  pl.BlockSpec((None, D_FF, D_MODEL), lambda i, be, nv: (be[i], 0, 0)),
                  pl.BlockSpec((None, 1, D_MODEL), lambda i, be, nv: (be[i], 0, 0))],
        out_specs=pl.BlockSpec((tm, D_MODEL), lambda i, be, nv: (i, 0)),
    )
    return pl.pallas_call(
        _moe_kernel,
        grid_spec=grid_spec,
        out_shape=jax.ShapeDtypeStruct((n_rows, D_MODEL), F32),
        compiler_params=_params(1),
        name="moe_experts",
    )(blk_expert, n_valid, xs, wgu, bgu, wd, bd)


def _moe(h2, logits, lw, tm):
    n_tok = h2.shape[0]
    idx_f, gate_f, rank_f, cnt_f = _route_call(logits)
    idx = idx_f[:, :TOP_K].astype(jnp.int32)
    gate = gate_f[:, :TOP_K]
    rank = rank_f[:, :TOP_K].astype(jnp.int32)
    counts = cnt_f[0, :N_EXPERTS].astype(jnp.int32)
    padded = (counts + tm - 1) // tm * tm
    pad_end = jnp.cumsum(padded)
    pad_start = pad_end - padded
    pos = pad_start[idx] + rank
    n_blocks = n_tok * TOP_K // tm + N_EXPERTS
    n_rows = n_blocks * tm
    tok = jnp.broadcast_to(jnp.arange(n_tok, dtype=jnp.int32)[:, None], (n_tok, TOP_K))
    row_tok = jnp.zeros((n_rows,), jnp.int32).at[pos.reshape(-1)].set(tok.reshape(-1))
    blk_start = jnp.arange(n_blocks, dtype=jnp.int32) * tm
    blk_expert = jnp.minimum(jnp.sum(pad_end[None, :] <= blk_start[:, None], axis=1), N_EXPERTS - 1).astype(jnp.int32)
    n_valid = (pad_end[-1] // tm).astype(jnp.int32).reshape(1)
    xs = jnp.take(h2, row_tok, axis=0)
    out = _moe_call(xs, blk_expert, n_valid, lw["wgu"], lw["bgu"], lw["wd"], lw["bd"], tm)
    picked = jnp.take(out, pos.reshape(-1), axis=0).reshape(n_tok, TOP_K, D_MODEL)
    return jnp.sum(picked * gate[:, :, None], axis=1)


def _cumsum_kernel(x_ref, o_ref, carry, *, tc):
    j = pl.program_id(1)

    @pl.when(j == 0)
    def _():
        carry[...] = jnp.zeros_like(carry)

    r_i = lax.broadcasted_iota(jnp.int32, (tc, tc), 0)
    c_i = lax.broadcasted_iota(jnp.int32, (tc, tc), 1)
    upto = jnp.where(r_i <= c_i, 1.0, 0.0)
    cs = jnp.dot(x_ref[...], upto, preferred_element_type=F32, precision=HIGHEST) + carry[...]
    o_ref[...] = cs
    carry[...] = cs[:, tc - 1:tc]


def _cumsum_call(x):
    rows, tk = x.shape
    tc = LANES
    tr = min(rows, LANES)
    spec = pl.BlockSpec((tr, tc), lambda r, j: (r, j))
    return pl.pallas_call(
        functools.partial(_cumsum_kernel, tc=tc),
        grid=(rows // tr, tk // tc),
        in_specs=[spec],
        out_specs=spec,
        out_shape=jax.ShapeDtypeStruct(x.shape, F32),
        scratch_shapes=[pltpu.VMEM((tr, 1), F32)],
        compiler_params=_params(2),
        name="logf_cumsum",
    )(x)


ATTN_K_ROWS = LANES
ATTN_V_ROWS = HEAD_DIM + 16


def _attn_kernel(q_ref, k_ref, v_ref, o_ref, acc_sc, *, tq, tk, pos0, nk):
    qi = pl.program_id(2)
    q_first = pos0 + qi * tq
    n_full = jnp.minimum(nk, (q_first + 1) // tk)
    n_end = jnp.minimum(nk, (q_first + tq - 1) // tk + 1)
    acc_sc[...] = jnp.zeros_like(acc_sc)
    width = KV_GROUP * tq

    def block(ki, m, masked):
        s = jnp.dot(k_ref[ki], q_ref[...], preferred_element_type=F32)
        if masked:
            key = ki * tk + lax.broadcasted_iota(jnp.int32, (tk, width), 0)
            qry = q_first + (lax.broadcasted_iota(jnp.int32, (tk, width), 1) & (tq - 1))
            s = jnp.where(key <= qry, s, NEG_INF)
        m_new = jnp.maximum(m, jnp.max(s, axis=0, keepdims=True))
        alpha = jnp.exp(m - m_new)
        p = jnp.exp(s - m_new).astype(BF16)
        acc_sc[...] = alpha * acc_sc[...] + jnp.dot(v_ref[ki], p, preferred_element_type=F32)
        return m_new

    m = jnp.full((1, width), NEG_INF, F32)
    m = lax.fori_loop(0, n_full, lambda ki, c: block(ki, c, False), m)
    lax.fori_loop(n_full, n_end, lambda ki, c: block(ki, c, True), m)
    a = acc_sc[...]
    o_ref[...] = (a[:HEAD_DIM] / a[HEAD_DIM:HEAD_DIM + 1]).astype(BF16)


def _attn_call(q_aug, k_aug, v_aug, pos0):
    bsz, _, nq, _, width = q_aug.shape
    nk, tk = k_aug.shape[2], k_aug.shape[3]
    tq = width // KV_GROUP
    assert tq & (tq - 1) == 0
    return pl.pallas_call(
        functools.partial(_attn_kernel, tq=tq, tk=tk, pos0=pos0, nk=nk),
        grid=(bsz, N_KV_HEADS, nq),
        in_specs=[pl.BlockSpec((None, None, None, ATTN_K_ROWS, width), lambda b, h, qi: (b, h, qi, 0, 0)),
                  pl.BlockSpec((None, None, nk, tk, ATTN_K_ROWS), lambda b, h, qi: (b, h, 0, 0, 0)),
                  pl.BlockSpec((None, None, nk, ATTN_V_ROWS, tk), lambda b, h, qi: (b, h, 0, 0, 0))],
        out_specs=pl.BlockSpec((None, None, None, HEAD_DIM, width), lambda b, h, qi: (b, h, qi, 0, 0)),
        out_shape=jax.ShapeDtypeStruct((bsz, N_KV_HEADS, nq, HEAD_DIM, width), BF16),
        scratch_shapes=[pltpu.VMEM((ATTN_V_ROWS, width), F32)],
        compiler_params=_params(3),
        name="forget_attention",
    )(q_aug, k_aug, v_aug)


def _split3(c):
    def top(x):
        bits = lax.bitcast_convert_type(x, jnp.uint32) & jnp.uint32(0xFFFF0000)
        return lax.bitcast_convert_type(bits, F32)
    hi = top(c)
    mid = top(c - hi)
    lo = top(c - hi - mid)
    return hi.astype(BF16), mid.astype(BF16), lo.astype(BF16)


def _attn_kv_operands(k_all, v_all, cum, tk):
    bsz, seq_k = k_all.shape[:2]
    nk = seq_k // tk
    kt = k_all.astype(BF16).transpose(0, 2, 1, 3)
    ck = jnp.stack(_split3(-cum.reshape(bsz, N_KV_HEADS, KV_GROUP, seq_k)), axis=3)
    ck = ck.reshape(bsz, N_KV_HEADS, 3 * KV_GROUP, seq_k).transpose(0, 1, 3, 2)
    ones = jnp.ones((bsz, N_KV_HEADS, seq_k, 3), BF16)
    zpad = jnp.zeros((bsz, N_KV_HEADS, seq_k, ATTN_K_ROWS - HEAD_DIM - 3 - 3 * KV_GROUP), BF16)
    k_aug = jnp.concatenate([kt, ones, ck, zpad], axis=3).reshape(bsz, N_KV_HEADS, nk, tk, ATTN_K_ROWS)
    vt = v_all.astype(BF16).transpose(0, 2, 3, 1)
    v_aug = jnp.concatenate([vt, jnp.ones((bsz, N_KV_HEADS, 1, seq_k), BF16),
                             jnp.zeros((bsz, N_KV_HEADS, ATTN_V_ROWS - HEAD_DIM - 1, seq_k), BF16)], axis=2)
    v_aug = v_aug.reshape(bsz, N_KV_HEADS, ATTN_V_ROWS, nk, tk).transpose(0, 1, 3, 2, 4)
    return k_aug, v_aug


def _attn_q_operand(q, cum_q, tq):
    bsz, seq = q.shape[:2]
    nq, width = seq // tq, KV_GROUP * tq
    lead = (bsz, N_KV_HEADS, nq)
    qt = q.reshape(bsz, nq, tq, N_KV_HEADS, KV_GROUP, HEAD_DIM).transpose(0, 3, 1, 5, 4, 2).reshape(*lead, HEAD_DIM, width)
    to_lanes = lambda c: c.reshape(bsz, N_KV_HEADS, KV_GROUP, nq, tq).transpose(0, 1, 3, 2, 4).reshape(*lead, 1, width)
    cq = [to_lanes(c) for c in _split3(cum_q)]
    sel = jnp.repeat(jnp.repeat(jnp.eye(KV_GROUP, dtype=BF16), 3, axis=0), tq, axis=1)
    sel = jnp.broadcast_to(sel, (*lead, 3 * KV_GROUP, width))
    zpad = jnp.zeros((*lead, ATTN_K_ROWS - HEAD_DIM - 3 - 3 * KV_GROUP, width), BF16)
    return jnp.concatenate([qt, *cq, sel, zpad], axis=3)


def _attn_untranspose(o_t, tq):
    bsz, _, nq = o_t.shape[:3]
    o = o_t.reshape(bsz, N_KV_HEADS, nq, HEAD_DIM, KV_GROUP, tq).transpose(0, 2, 5, 1, 4, 3)
    return o.reshape(bsz * nq * tq, D_MODEL)


def _split_mods(m, n, expand_t):
    bsz = m.shape[0]
    m = m.reshape(bsz, n, 1, D_MODEL)
    if expand_t:
        m = jnp.broadcast_to(m.transpose(1, 0, 2, 3), (n, bsz, expand_t, D_MODEL)).reshape(1, n, bsz * expand_t, D_MODEL)
    return m


def _trunk(x, ada, ada_kv, ada_final, pool_hist, kv_past, pos0, w):
    bsz, seq, _ = x.shape
    n_tok = bsz * seq
    flat_small = seq < TOKEN_TILE
    expand_t = seq if flat_small else 0
    tiles_per_mod = 1 if flat_small else seq // TOKEN_TILE
    tm = MOE_TILE if n_tok * TOP_K >= 64 * MOE_TILE else 128
    hist16 = jnp.pad(pool_hist, ((0, 0), (0, 0), (1, 0), (0, 0)))
    new_pool = []
    pending = None
    flat_prev = lambda p: None if p is None else (p[0], _split_mods(p[1], 1, expand_t))
    xf = x.reshape(n_tok, D_MODEL)
    k4 = v4 = logf_new = k_aug = v_aug = cum_q = None
    for layer in range(DEPTH):
        lw = w["layers"][layer]
        m6 = ada[layer]
        if layer < N_A_LAYERS:
            mods = _split_mods(m6[:, :5 * D_MODEL], 5, 0)
            pprev = None
            if pending is not None:
                pprev = (pending[0].reshape(bsz, seq, D_MODEL), _split_mods(pending[1], 1, 0))
            x1, h2, logits, hout = _pool_call(
                xf.reshape(bsz, seq, D_MODEL), pprev, hist16[layer], mods, lw["gm"], lw["gf"],
                lw["w_pool"], lw["pool_scale"], lw["wr"], lw["br"], pos0)
            new_pool.append(hout[:, 1:])
            x1 = x1.reshape(n_tok, D_MODEL)
            h2 = h2.reshape(n_tok, D_MODEL)
            logits = logits.reshape(n_tok, LANES)
        else:
            mods = _split_mods(m6[:, :5 * D_MODEL], 5, expand_t)
            q = _bpre_call(xf, flat_prev(pending), mods[:, 0:2], lw["gm"], lw["wq"], tiles_per_mod)
            tq = min(seq, ATTN_TQ)
            o_t = _attn_call(_attn_q_operand(q.reshape(bsz, seq, D_MODEL), cum_q, tq), k_aug, v_aug, pos0)
            o = _attn_untranspose(o_t, tq)
            x1, h2, logits = _bpost_call(xf, flat_prev(pending), o, mods[:, 2:5], lw["wo"],
                                         lw["gf"], lw["wr"], lw["br"], tiles_per_mod)
        y = _moe(h2, logits, lw, tm)
        xf, pending = x1, (y, m6[:, 5 * D_MODEL:])
        if layer == N_A_LAYERS - 1:
            mods_kv = _split_mods(ada_kv, 2, expand_t)
            xf, k_new, v_new, lf = _kv_call(xf, flat_prev(pending), mods_kv, w["g_kv"], w["wk"], w["wv"], w["wf"],
                                            w["bf"], tiles_per_mod)
            pending = None
            logf_new = lf[:, :N_HEADS].reshape(bsz, seq, N_HEADS)
            k4 = k_new.reshape(bsz, seq, N_KV_HEADS, HEAD_DIM)
            v4 = v_new.reshape(bsz, seq, N_KV_HEADS, HEAD_DIM)
            if kv_past is None:
                k_all, v_all, logf_all = k4, v4, logf_new
            else:
                k_all = jnp.concatenate([kv_past[0], k4], axis=1)
                v_all = jnp.concatenate([kv_past[1], v4], axis=1)
                logf_all = jnp.concatenate([kv_past[2], logf_new], axis=1)
            seq_k = k_all.shape[1]
            seq_kp = -(-seq_k // LANES) * LANES
            padk = ((0, 0), (0, seq_kp - seq_k), (0, 0), (0, 0))
            lf_t = jnp.pad(logf_all, padk[:3]).transpose(0, 2, 1).reshape(bsz * N_HEADS, seq_kp)
            cum = _cumsum_call(lf_t).reshape(bsz, N_HEADS, seq_kp)
            cum_q = cum[:, :, seq_k - seq:seq_k]
            tk = ATTN_TK if (seq >= ATTN_TQ and seq_kp % ATTN_TK == 0) else seq_kp
            k_aug, v_aug = _attn_kv_operands(jnp.pad(k_all, padk), jnp.pad(v_all, padk), cum, tk)
    mods_f = _split_mods(ada_final, 2, expand_t)
    y_out = _final_call(xf, flat_prev(pending), mods_f, w["g_final"], tiles_per_mod)
    return (y_out.reshape(bsz, seq, D_MODEL), k4, v4, logf_new, jnp.stack(new_pool, axis=0))


def _pad_lanes(a, axis):
    pad = [(0, 0)] * a.ndim
    pad[axis] = (0, LANES - a.shape[axis])
    return jnp.pad(a, pad)


def kernel(x_prompt, x_sample, cache_k, cache_v, cache_logf, state_pool, c_prompt, c_sample, norm_mix_g, norm_ffn_g, w_ada, b_ada, w_pool, pool_scale, norm_kv_g, w_ada_kv, b_ada_kv, w_kv, b_forget, w_q, w_o, w_router, b_router, w_gate_up, b_gate_up, w_down, b_down, norm_final_g, w_ada_final, b_ada_final):
    bp, bs = c_prompt.shape[0], c_sample.shape[0]
    c_all = jnp.concatenate([c_prompt, c_sample], axis=0)
    ada = _ada_call(c_all, w_ada, b_ada)
    w_small = jnp.stack([w_ada_kv, w_ada_final])
    b_small = jnp.stack([b_ada_kv, b_ada_final])
    ada_small = _ada_call(c_all, w_small, b_small)

    layers = []
    for layer in range(DEPTH):
        lw = dict(gm=norm_mix_g[layer][None], gf=norm_ffn_g[layer][None],
                  wr=_pad_lanes(w_router[layer], 1), br=_pad_lanes(b_router[layer][None], 1),
                  wgu=w_gate_up[layer].astype(BF16), bgu=b_gate_up[layer][:, None, :],
                  wd=w_down[layer].astype(BF16), bd=b_down[layer][:, None, :])
        if layer < N_A_LAYERS:
            lw.update(w_pool=w_pool[layer].astype(BF16), pool_scale=pool_scale[layer][None])
        else:
            j = layer - N_A_LAYERS
            lw.update(wq=w_q[j].astype(BF16), wo=w_o[j].astype(BF16))
        layers.append(lw)
    w = dict(layers=layers, g_kv=norm_kv_g[None], g_final=norm_final_g[None],
             wk=w_kv[:, :KV_WIDTH].astype(BF16), wv=w_kv[:, KV_WIDTH:2 * KV_WIDTH].astype(BF16),
             wf=_pad_lanes(w_kv[:, 2 * KV_WIDTH:], 1), bf=_pad_lanes(b_forget[None], 1))

    zero_hist = jnp.zeros((N_A_LAYERS, bp, POOL_HIST, D_MODEL), F32)
    y_p, k_p, v_p, lf_p, pool_p = _trunk(x_prompt, ada[:, :bp], ada_small[0, :bp], ada_small[1, :bp],
                                         zero_hist, None, 0, w)
    y_s, k_s, v_s, lf_s, pool_s = _trunk(x_sample, ada[:, bp:], ada_small[0, bp:], ada_small[1, bp:],
                                         state_pool, (cache_k, cache_v, cache_logf), cache_k.shape[1], w)
    return (y_p, y_s, k_p, v_p, lf_p, pool_p, k_s, v_s, lf_s, pool_s)
```

```python
import functools

import jax
import jax.numpy as jnp
from jax import lax
from jax.experimental import pallas as pl
from jax.experimental.pallas import tpu as pltpu

F32 = jnp.float32
BF16 = jnp.bfloat16
HIGHEST = lax.Precision.HIGHEST

D_MODEL = 1024
DEPTH = 4
N_A_LAYERS = DEPTH // 2
POOL_WINDOWS = (2, 4, 8, 16)
POOL_GROUP_DIM = D_MODEL // len(POOL_WINDOWS)
POOL_HIST = max(POOL_WINDOWS) - 1
HIST_ROWS = POOL_HIST + 1
N_HEADS = 16
HEAD_DIM = D_MODEL // N_HEADS
N_KV_HEADS = 4
KV_GROUP = N_HEADS // N_KV_HEADS
KV_WIDTH = N_KV_HEADS * HEAD_DIM
N_EXPERTS = 32
TOP_K = 4
D_FF = D_MODEL
SWIGLU_LIMIT = 7.0
SWIGLU_ALPHA = 1.702
RMS_EPS = 1e-6
NEG_INF = -1e30

LANES = 128
SUBLANES = 8
ROW_TILES = D_MODEL // LANES
TOKEN_TILE = 512
MOE_TILE = 256
FF_CHUNK = 512
DISPATCH_CHUNK = 2048
COMBINE_TILE = 256
DMA_UNROLL = 32
ATTN_TQ = 256
ATTN_TK = 256
VMEM_LIMIT = 56 * 1024 * 1024

assert ROW_TILES == SUBLANES


def _params(n_grid_dims):
    return pltpu.CompilerParams(dimension_semantics=("arbitrary",) * n_grid_dims, vmem_limit_bytes=VMEM_LIMIT)


def _norm_mod(x, gain, shift, scale):
    y = x * lax.rsqrt(jnp.mean(x * x, axis=-1, keepdims=True) + RMS_EPS) * gain
    return y * (1.0 + scale) + shift


def _store_row_tiles(ref, val):
    rows = val.shape[0]
    for j in range(ROW_TILES):
        ref[pl.ds(j, rows, stride=ROW_TILES), :] = val[:, j * LANES:(j + 1) * LANES]


def _load_row_tiles(ref, rows, first=0, stride=ROW_TILES):
    return jnp.concatenate([ref[pl.ds(first + j, rows, stride=stride), :] for j in range(ROW_TILES)], axis=1)


def _ada_kernel(c_ref, w_ref, b_ref, o_ref):
    c = c_ref[...]
    s = c * jax.nn.sigmoid(c)
    o_ref[...] = jnp.dot(s, w_ref[...], preferred_element_type=F32, precision=HIGHEST) + b_ref[...]


def _ada_call(c, w, b):
    n_layers, _, m = w.shape
    bc = c.shape[0]
    tn = 1024
    return pl.pallas_call(
        _ada_kernel,
        grid=(n_layers, m // tn),
        in_specs=[pl.BlockSpec((bc, D_MODEL), lambda l, j: (0, 0)),
                  pl.BlockSpec((None, D_MODEL, tn), lambda l, j: (l, 0, j)),
                  pl.BlockSpec((None, 1, tn), lambda l, j: (l, 0, j))],
        out_specs=pl.BlockSpec((None, bc, tn), lambda l, j: (l, 0, j)),
        out_shape=jax.ShapeDtypeStruct((n_layers, bc, m), F32),
        compiler_params=_params(2),
        name="ada",
    )(c, w, b.reshape(n_layers, 1, m))


def _tail(x1, gf_ref, sh2, sc2, wr_ref, br_ref, x1_ref, h2_ref, lg_ref):
    h2 = _norm_mod(x1, gf_ref[...], sh2, sc2)
    x1_ref[...] = x1
    _store_row_tiles(h2_ref, h2)
    lg_ref[...] = jnp.dot(h2, wr_ref[...], preferred_element_type=F32, precision=HIGHEST) + br_ref[...]


def _pool_kernel(x_ref, hist_ref, mod_ref, gm_ref, gf_ref, wp_ref, ps_ref, wr_ref, br_ref,
                 x1_ref, h2_ref, lg_ref, hout_ref, ext, *, tt, pos0):
    t = pl.program_id(1)
    x = x_ref[...]
    h = _norm_mod(x, gm_ref[...], mod_ref[0], mod_ref[1])

    @pl.when(t == 0)
    def _():
        ext[0:HIST_ROWS, :] = hist_ref[...]

    ext[HIST_ROWS:HIST_ROWS + tt, :] = h
    pos = pos0 + t * tt + lax.broadcasted_iota(jnp.int32, (tt, 1), 0)
    ys = []
    for g, w in enumerate(POOL_WINDOWS):
        lo, hi = g * POOL_GROUP_DIM, (g + 1) * POOL_GROUP_DIM
        hg = h[:, lo:hi]
        win = hg
        for j in range(1, w):
            win = win + ext[HIST_ROWS - j:HIST_ROWS - j + tt, lo:hi]
        inv_cnt = 1.0 / jnp.minimum(pos + 1, w).astype(F32)
        u = win * inv_cnt - hg
        ys.append(jnp.dot(u.astype(BF16), wp_ref[g], preferred_element_type=F32))
    y = jnp.concatenate(ys, axis=-1) * ps_ref[...]
    x1 = x + mod_ref[2] * y
    last = ext[tt:tt + HIST_ROWS, :]
    ext[0:HIST_ROWS, :] = last
    hout_ref[...] = last
    _tail(x1, gf_ref, mod_ref[3], mod_ref[4], wr_ref, br_ref, x1_ref, h2_ref, lg_ref)


def _pool_call(x, hist16, mods, gm, gf, wp, ps, wr, br, pos0):
    bsz, seq, _ = x.shape
    tt = min(seq, TOKEN_TILE)
    nt = seq // tt
    row = lambda b, t: (b, t, 0)
    flat = lambda b, t: (b * nt + t, 0)
    const2 = lambda b, t: (0, 0)
    xspec = pl.BlockSpec((None, tt, D_MODEL), row)
    specs = [xspec,
             pl.BlockSpec((None, HIST_ROWS, D_MODEL), lambda b, t: (b, 0, 0)),
             pl.BlockSpec((None, mods.shape[1], 1, D_MODEL), lambda b, t: (b, 0, 0, 0)),
             pl.BlockSpec((1, D_MODEL), const2), pl.BlockSpec((1, D_MODEL), const2),
             pl.BlockSpec(wp.shape, lambda b, t: (0, 0, 0)),
             pl.BlockSpec((1, D_MODEL), const2),
             pl.BlockSpec((D_MODEL, LANES), const2), pl.BlockSpec((1, LANES), const2)]
    return pl.pallas_call(
        functools.partial(_pool_kernel, tt=tt, pos0=pos0),
        grid=(bsz, nt),
        in_specs=specs,
        out_specs=[xspec, pl.BlockSpec((tt * ROW_TILES, LANES), flat), pl.BlockSpec((tt, LANES), flat),
                   pl.BlockSpec((None, HIST_ROWS, D_MODEL), lambda b, t: (b, 0, 0))],
        out_shape=[jax.ShapeDtypeStruct((bsz, seq, D_MODEL), F32),
                   jax.ShapeDtypeStruct((bsz * seq * ROW_TILES, LANES), F32),
                   jax.ShapeDtypeStruct((bsz * seq, LANES), F32),
                   jax.ShapeDtypeStruct((bsz, HIST_ROWS, D_MODEL), F32)],
        scratch_shapes=[pltpu.VMEM((HIST_ROWS + tt, D_MODEL), F32)],
        compiler_params=_params(2),
        name="pool_layer",
    )(x, hist16, mods, gm, gf, wp, ps, wr, br)


def _tok_specs(tt, mods, tiles_per_mod):
    nmod, rows = mods.shape[1], mods.shape[2]
    xspec = pl.BlockSpec((tt, D_MODEL), lambda i: (i, 0))
    mspec = pl.BlockSpec((None, nmod, rows, D_MODEL), lambda i: (i // tiles_per_mod, 0, 0, 0))
    return xspec, mspec


def _kv_kernel(x_ref, mod_ref, g_ref, wk_ref, wv_ref, wf_ref, bf_ref, k_out, v_out, lf_out):
    h = _norm_mod(x_ref[...], g_ref[...], mod_ref[0], mod_ref[1])
    hb = h.astype(BF16)
    k_out[...] = jnp.dot(hb, wk_ref[...], preferred_element_type=F32)
    v_out[...] = jnp.dot(hb, wv_ref[...], preferred_element_type=F32)
    z = jnp.dot(h, wf_ref[...], preferred_element_type=F32, precision=HIGHEST) + bf_ref[...]
    lf_out[...] = jax.nn.log_sigmoid(z)


def _kv_call(x, mods, g, wk, wv, wf, bf, tiles_per_mod):
    n_tok = x.shape[0]
    tt = min(n_tok, TOKEN_TILE)
    xspec, mspec = _tok_specs(tt, mods, tiles_per_mod)
    const = lambda i: (0, 0)
    kvspec = pl.BlockSpec((tt, KV_WIDTH), lambda i: (i, 0))
    return pl.pallas_call(
        _kv_kernel,
        grid=(n_tok // tt,),
        in_specs=[xspec, mspec, pl.BlockSpec((1, D_MODEL), const),
                  pl.BlockSpec((D_MODEL, KV_WIDTH), const), pl.BlockSpec((D_MODEL, KV_WIDTH), const),
                  pl.BlockSpec((D_MODEL, LANES), const), pl.BlockSpec((1, LANES), const)],
        out_specs=[kvspec, kvspec, pl.BlockSpec((tt, LANES), lambda i: (i, 0))],
        out_shape=[jax.ShapeDtypeStruct((n_tok, KV_WIDTH), F32),
                   jax.ShapeDtypeStruct((n_tok, KV_WIDTH), F32),
                   jax.ShapeDtypeStruct((n_tok, LANES), F32)],
        compiler_params=_params(1),
        name="kv_proj",
    )(x, mods, g, wk, wv, wf, bf)


def _bpre_kernel(x_ref, mod_ref, g_ref, wq_ref, q_out):
    h = _norm_mod(x_ref[...], g_ref[...], mod_ref[0], mod_ref[1])
    q = jnp.dot(h.astype(BF16), wq_ref[...], preferred_element_type=F32)
    q_out[...] = (q * (HEAD_DIM ** -0.5)).astype(BF16)


def _bpre_call(x, mods, g, wq, tiles_per_mod):
    n_tok = x.shape[0]
    tt = min(n_tok, TOKEN_TILE)
    xspec, mspec = _tok_specs(tt, mods, tiles_per_mod)
    const = lambda i: (0, 0)
    return pl.pallas_call(
        _bpre_kernel,
        grid=(n_tok // tt,),
        in_specs=[xspec, mspec, pl.BlockSpec((1, D_MODEL), const), pl.BlockSpec((D_MODEL, D_MODEL), const)],
        out_specs=xspec,
        out_shape=jax.ShapeDtypeStruct((n_tok, D_MODEL), BF16),
        compiler_params=_params(1),
        name="q_proj",
    )(x, mods, g, wq)


def _bpost_kernel(x_ref, o_ref, mod_ref, wo_ref, gf_ref, wr_ref, br_ref, x1_ref, h2_ref, lg_ref):
    y = jnp.dot(o_ref[...], wo_ref[...], preferred_element_type=F32)
    x1 = x_ref[...] + mod_ref[0] * y
    _tail(x1, gf_ref, mod_ref[1], mod_ref[2], wr_ref, br_ref, x1_ref, h2_ref, lg_ref)


def _bpost_call(x, o, mods, wo, gf, wr, br, tiles_per_mod):
    n_tok = x.shape[0]
    tt = min(n_tok, TOKEN_TILE)
    xspec, mspec = _tok_specs(tt, mods, tiles_per_mod)
    const = lambda i: (0, 0)
    return pl.pallas_call(
        _bpost_kernel,
        grid=(n_tok // tt,),
        in_specs=[xspec, xspec, mspec, pl.BlockSpec((D_MODEL, D_MODEL), const), pl.BlockSpec((1, D_MODEL), const),
                  pl.BlockSpec((D_MODEL, LANES), const), pl.BlockSpec((1, LANES), const)],
        out_specs=[xspec, pl.BlockSpec((tt * ROW_TILES, LANES), lambda i: (i, 0)),
                   pl.BlockSpec((tt, LANES), lambda i: (i, 0))],
        out_shape=[jax.ShapeDtypeStruct((n_tok, D_MODEL), F32),
                   jax.ShapeDtypeStruct((n_tok * ROW_TILES, LANES), F32),
                   jax.ShapeDtypeStruct((n_tok, LANES), F32)],
        compiler_params=_params(1),
        name="o_proj_tail",
    )(x, o, mods, wo, gf, wr, br)


def _final_kernel(x_ref, mod_ref, g_ref, y_out):
    y_out[...] = _norm_mod(x_ref[...], g_ref[...], mod_ref[0], mod_ref[1])


def _final_call(x, mods, g, tiles_per_mod):
    n_tok = x.shape[0]
    tt = min(n_tok, TOKEN_TILE)
    xspec, mspec = _tok_specs(tt, mods, tiles_per_mod)
    return pl.pallas_call(
        _final_kernel,
        grid=(n_tok // tt,),
        in_specs=[xspec, mspec, pl.BlockSpec((1, D_MODEL), lambda i: (0, 0))],
        out_specs=xspec,
        out_shape=jax.ShapeDtypeStruct((n_tok, D_MODEL), F32),
        compiler_params=_params(1),
        name="final_norm",
    )(x, mods, g)


def _route_kernel(lg_ref, idx_ref, gate_ref, rank_ref, cnt_ref, carry, *, tr):
    i = pl.program_id(0)

    @pl.when(i == 0)
    def _():
        carry[...] = jnp.zeros_like(carry)

    lane = lax.broadcasted_iota(jnp.int32, (tr, LANES), 1)
    lanef = lane.astype(F32)
    logit = jnp.where(lane < N_EXPERTS, lg_ref[...], -jnp.inf)
    r_i = lax.broadcasted_iota(jnp.int32, (tr, tr), 0)
    c_i = lax.broadcasted_iota(jnp.int32, (tr, tr), 1)
    earlier = jnp.where(c_i < r_i, 1.0, 0.0).astype(BF16)
    base = carry[...]
    idx_out = jnp.zeros((tr, LANES), F32)
    rank_out = jnp.zeros((tr, LANES), F32)
    val_out = jnp.zeros((tr, LANES), F32)
    top0 = None
    for k in range(TOP_K):
        m = jnp.max(logit, axis=1, keepdims=True)
        sel = jnp.min(jnp.where(logit == m, lanef, float(LANES)), axis=1, keepdims=True)
        onehot = lanef == sel
        ohf = jnp.where(onehot, 1.0, 0.0)
        before = jnp.dot(earlier, ohf.astype(BF16), preferred_element_type=F32)
        rank = jnp.sum(jnp.where(onehot, before + base, 0.0), axis=1, keepdims=True)
        base = base + jnp.sum(ohf, axis=0, keepdims=True)
        logit = jnp.where(onehot, -jnp.inf, logit)
        if k == 0:
            top0 = m
        idx_out = jnp.where(lane == k, sel, idx_out)
        rank_out = jnp.where(lane == k, rank, rank_out)
        val_out = jnp.where(lane == k, jnp.exp(m - top0), val_out)
    carry[...] = base
    cnt_ref[...] = base
    idx_ref[...] = idx_out
    rank_ref[...] = rank_out
    gate_ref[...] = val_out / jnp.sum(val_out, axis=1, keepdims=True)


def _route_call(logits):
    n_tok = logits.shape[0]
    tr = min(n_tok, TOKEN_TILE)
    spec = pl.BlockSpec((tr, LANES), lambda i: (i, 0))
    sds = jax.ShapeDtypeStruct((n_tok, LANES), F32)
    return pl.pallas_call(
        functools.partial(_route_kernel, tr=tr),
        grid=(n_tok // tr,),
        in_specs=[spec],
        out_specs=[spec, spec, spec, pl.BlockSpec((1, LANES), lambda i: (0, 0))],
        out_shape=[sds, sds, sds, jax.ShapeDtypeStruct((1, LANES), F32)],
        scratch_shapes=[pltpu.VMEM((1, LANES), F32)],
        compiler_params=_params(1),
        name="route",
    )(logits)


def _row_copy(src_hbm, src_row, dst_hbm, dst_row, sem):
    return pltpu.make_async_copy(src_hbm.at[pl.ds(pl.multiple_of(src_row, ROW_TILES), ROW_TILES), :],
                                 dst_hbm.at[pl.ds(pl.multiple_of(dst_row, ROW_TILES), ROW_TILES), :], sem)


def _dispatch_kernel(dst_ref, h_hbm, xs_in_hbm, xs_hbm, sem, *, chunk):
    del xs_in_hbm
    first_tok = pl.program_id(0) * (chunk // TOP_K)

    def issue(g, carry):
        tok = first_tok + g * (DMA_UNROLL // TOP_K)
        for u in range(DMA_UNROLL):
            _row_copy(h_hbm, (tok + u // TOP_K) * ROW_TILES, xs_hbm, dst_ref[0, g * DMA_UNROLL + u], sem).start()
        return carry

    def drain(g, carry):
        for u in range(DMA_UNROLL):
            _row_copy(h_hbm, 0, xs_hbm, 0, sem).wait()
        return carry

    lax.fori_loop(0, chunk // DMA_UNROLL, issue, 0)
    lax.fori_loop(0, chunk // DMA_UNROLL, drain, 0)


def _dispatch_call(h2_tiles, dst_rows, xs):
    n_asg = dst_rows.shape[0]
    chunk = min(n_asg, DISPATCH_CHUNK)
    any_spec = pl.BlockSpec(memory_space=pl.ANY)
    return pl.pallas_call(
        functools.partial(_dispatch_kernel, chunk=chunk),
        grid=(n_asg // chunk,),
        in_specs=[pl.BlockSpec((None, 1, chunk), lambda i: (i, 0, 0), memory_space=pltpu.SMEM), any_spec, any_spec],
        out_specs=any_spec,
        out_shape=jax.ShapeDtypeStruct(xs.shape, F32),
        scratch_shapes=[pltpu.SemaphoreType.DMA(())],
        input_output_aliases={2: 0},
        compiler_params=_params(1),
        name="moe_dispatch",
    )(dst_rows.reshape(n_asg // chunk, 1, chunk), h2_tiles, xs)


def _moe_kernel(be_ref, nv_ref, x_ref, wgu_ref, bgu_ref, wd_ref, bd_ref, o_ref, wgu_bf, wd_bf, *, tm):
    i = pl.program_id(0)
    active = i < nv_ref[0]
    fresh = jnp.logical_or(i == 0, be_ref[i] != be_ref[jnp.maximum(i - 1, 0)])

    @pl.when(jnp.logical_and(active, fresh))
    def _():
        for c in range(2 * D_FF // FF_CHUNK):
            wgu_bf[:, c * FF_CHUNK:(c + 1) * FF_CHUNK] = wgu_ref[:, c * FF_CHUNK:(c + 1) * FF_CHUNK].astype(BF16)
        for c in range(D_MODEL // FF_CHUNK):
            wd_bf[:, c * FF_CHUNK:(c + 1) * FF_CHUNK] = wd_ref[:, c * FF_CHUNK:(c + 1) * FF_CHUNK].astype(BF16)

    @pl.when(active)
    def _():
        x = _load_row_tiles(x_ref, tm).astype(BF16)
        acc = None
        for c in range(D_FF // FF_CHUNK):
            lo = c * FF_CHUNK
            glu = jnp.dot(x, wgu_bf[:, lo:lo + FF_CHUNK], preferred_element_type=F32) + bgu_ref[:, lo:lo + FF_CHUNK]
            lin = (jnp.dot(x, wgu_bf[:, D_FF + lo:D_FF + lo + FF_CHUNK], preferred_element_type=F32)
                   + bgu_ref[:, D_FF + lo:D_FF + lo + FF_CHUNK])
            glu = jnp.minimum(glu, SWIGLU_LIMIT)
            lin = jnp.clip(lin, -SWIGLU_LIMIT, SWIGLU_LIMIT)
            act = glu * jax.nn.sigmoid(SWIGLU_ALPHA * glu) * (lin + 1.0)
            part = jnp.dot(act.astype(BF16), wd_bf[lo:lo + FF_CHUNK, :], preferred_element_type=F32)
            acc = part if acc is None else acc + part
        _store_row_tiles(o_ref, acc + bd_ref[...])

    @pl.when(jnp.logical_not(active))
    def _():
        o_ref[...] = jnp.zeros_like(o_ref)


def _moe_call(xs, blk_expert, n_valid, wgu, bgu, wd, bd, tm):
    n_blocks = blk_expert.shape[0]
    rspec = pl.BlockSpec((tm * ROW_TILES, LANES), lambda i, be, nv: (i, 0))
    grid_spec = pltpu.PrefetchScalarGridSpec(
        num_scalar_prefetch=2,
        grid=(n_blocks,),
        in_specs=[rspec,
                  pl.BlockSpec((None, D_MODEL, 2 * D_FF), lambda i, be, nv: (be[i], 0, 0)),
                  pl.BlockSpec((None, 1, 2 * D_FF), lambda i, be, nv: (be[i], 0, 0)),
                  pl.BlockSpec((None, D_FF, D_MODEL), lambda i, be, nv: (be[i], 0, 0)),
                  pl.BlockSpec((None, 1, D_MODEL), lambda i, be, nv: (be[i], 0, 0))],
        out_specs=rspec,
        scratch_shapes=[pltpu.VMEM((D_MODEL, 2 * D_FF), BF16), pltpu.VMEM((D_FF, D_MODEL), BF16)],
    )
    return pl.pallas_call(
        functools.partial(_moe_kernel, tm=tm),
        grid_spec=grid_spec,
        out_shape=jax.ShapeDtypeStruct((n_blocks * tm * ROW_TILES, LANES), F32),
        compiler_params=_params(1),
        name="moe_experts",
    )(blk_expert, n_valid, xs, wgu, bgu, wd, bd)


def _combine_kernel(cur_ref, nxt_ref, x_ref, gate_ref, g2_ref, out_hbm, x2_ref, buf, sems, *, tt, n_tiles):
    i = pl.program_id(0)
    n_copy = tt * TOP_K

    def gather(rows_ref, slot, wait):
        def body(g, carry):
            tok = g * (DMA_UNROLL // TOP_K)
            for u in range(DMA_UNROLL):
                if wait:
                    _row_copy(out_hbm, 0, buf.at[slot], 0, sems.at[slot]).wait()
                else:
                    dst = ((u % TOP_K) * tt + tok + u // TOP_K) * ROW_TILES
                    _row_copy(out_hbm, rows_ref[0, g * DMA_UNROLL + u], buf.at[slot], dst, sems.at[slot]).start()
            return carry
        lax.fori_loop(0, n_copy // DMA_UNROLL, body, 0)

    slot = i % 2

    @pl.when(i == 0)
    def _():
        gather(cur_ref, 0, False)

    @pl.when(i + 1 < n_tiles)
    def _():
        gather(nxt_ref, 1 - slot, False)

    gather(cur_ref, slot, True)
    gates = gate_ref[...]
    gate_k = [jnp.broadcast_to(gates[:, k:k + 1], (tt, LANES)) for k in range(TOP_K)]
    rows = buf.at[slot]
    for j in range(ROW_TILES):
        cols = slice(j * LANES, (j + 1) * LANES)
        y = None
        for k in range(TOP_K):
            term = gate_k[k] * rows[pl.ds(k * tt * ROW_TILES + j, tt, stride=ROW_TILES), :]
            y = term if y is None else y + term
        x2_ref[:, cols] = x_ref[:, cols] + g2_ref[0, :, cols] * y


def _combine_call(x1, gate, g2, out_tiles, pos_rows, tiles_per_mod_512):
    n_tok = x1.shape[0]
    tt = min(n_tok, COMBINE_TILE)
    n_tiles = n_tok // tt
    rows_per_mod = g2.shape[2]
    tiles_per_mod = tiles_per_mod_512 * (TOKEN_TILE // tt) if rows_per_mod == 1 else 1
    if rows_per_mod == 1:
        g2spec = pl.BlockSpec((None, 1, 1, D_MODEL), lambda i: (i // tiles_per_mod, 0, 0, 0))
    else:
        g2spec = pl.BlockSpec((None, 1, tt, D_MODEL), lambda i: (0, 0, i, 0))
    pos3 = pos_rows.reshape(n_tiles, 1, tt * TOP_K)
    pspec = lambda f: pl.BlockSpec((None, 1, tt * TOP_K), f, memory_space=pltpu.SMEM)
    return pl.pallas_call(
        functools.partial(_combine_kernel, tt=tt, n_tiles=n_tiles),
        grid=(n_tiles,),
        in_specs=[pspec(lambda i: (i, 0, 0)), pspec(lambda i: (jnp.minimum(i + 1, n_tiles - 1), 0, 0)),
                  pl.BlockSpec((tt, D_MODEL), lambda i: (i, 0)), pl.BlockSpec((tt, LANES), lambda i: (i, 0)),
                  g2spec, pl.BlockSpec(memory_space=pl.ANY)],
        out_specs=pl.BlockSpec((tt, D_MODEL), lambda i: (i, 0)),
        out_shape=jax.ShapeDtypeStruct((n_tok, D_MODEL), F32),
        scratch_shapes=[pltpu.VMEM((2, tt * TOP_K * ROW_TILES, LANES), F32), pltpu.SemaphoreType.DMA((2,))],
        compiler_params=_params(1),
        name="moe_combine",
    )(pos3, pos3, x1, gate, g2, out_tiles)


def _moe(x1, h2_tiles, logits, g2, xs, lw, tm, tiles_per_mod):
    n_tok = x1.shape[0]
    idx_f, gate_f, rank_f, cnt_f = _route_call(logits)
    idx = idx_f[:, :TOP_K].astype(jnp.int32)
    rank = rank_f[:, :TOP_K].astype(jnp.int32)
    counts = cnt_f[0, :N_EXPERTS].astype(jnp.int32)
    padded = (counts + tm - 1) // tm * tm
    pad_end = jnp.cumsum(padded)
    pad_start = pad_end - padded
    pos_rows = ((pad_start[idx] + rank) * ROW_TILES).reshape(-1)
    n_blocks = n_tok * TOP_K // tm + N_EXPERTS
    blk_start = jnp.arange(n_blocks, dtype=jnp.int32) * tm
    blk_expert = jnp.minimum(jnp.sum(pad_end[None, :] <= blk_start[:, None], axis=1), N_EXPERTS - 1).astype(jnp.int32)
    n_valid = (pad_end[-1] // tm).astype(jnp.int32).reshape(1)
    xs = _dispatch_call(h2_tiles, pos_rows, xs)
    out = _moe_call(xs, blk_expert, n_valid, lw["wgu"], lw["bgu"], lw["wd"], lw["bd"], tm)
    return _combine_call(x1, gate_f, g2, out, pos_rows, tiles_per_mod), xs


def _cumsum_kernel(x_ref, o_ref, carry, *, tc):
    j = pl.program_id(1)

    @pl.when(j == 0)
    def _():
        carry[...] = jnp.zeros_like(carry)

    r_i = lax.broadcasted_iota(jnp.int32, (tc, tc), 0)
    c_i = lax.broadcasted_iota(jnp.int32, (tc, tc), 1)
    upto = jnp.where(r_i <= c_i, 1.0, 0.0)
    cs = jnp.dot(x_ref[...], upto, preferred_element_type=F32, precision=HIGHEST) + carry[...]
    o_ref[...] = cs
    carry[...] = cs[:, tc - 1:tc]


def _cumsum_call(x):
    rows, tk = x.shape
    tc = LANES
    tr = min(rows, LANES)
    spec = pl.BlockSpec((tr, tc), lambda r, j: (r, j))
    return pl.pallas_call(
        functools.partial(_cumsum_kernel, tc=tc),
        grid=(rows // tr, tk // tc),
        in_specs=[spec],
        out_specs=spec,
        out_shape=jax.ShapeDtypeStruct(x.shape, F32),
        scratch_shapes=[pltpu.VMEM((tr, 1), F32)],
        compiler_params=_params(2),
        name="logf_cumsum",
    )(x)


ATTN_K_ROWS = LANES
ATTN_V_ROWS = HEAD_DIM + 16


def _attn_kernel(q_ref, k_ref, v_ref, o_ref, acc_sc, *, tq, tk, pos0, nk):
    qi = pl.program_id(2)
    q_first = pos0 + qi * tq
    n_full = jnp.minimum(nk, (q_first + 1) // tk)
    n_end = jnp.minimum(nk, (q_first + tq - 1) // tk + 1)
    acc_sc[...] = jnp.zeros_like(acc_sc)
    width = KV_GROUP * tq

    def block(ki, m, masked):
        s = jnp.dot(k_ref[ki], q_ref[...], preferred_element_type=F32)
        if masked:
            key = ki * tk + lax.broadcasted_iota(jnp.int32, (tk, width), 0)
            qry = q_first + (lax.broadcasted_iota(jnp.int32, (tk, width), 1) & (tq - 1))
            s = jnp.where(key <= qry, s, NEG_INF)
        m_new = jnp.maximum(m, jnp.max(s, axis=0, keepdims=True))
        alpha = jnp.exp(m - m_new)
        p = jnp.exp(s - m_new).astype(BF16)
        acc_sc[...] = alpha * acc_sc[...] + jnp.dot(v_ref[ki], p, preferred_element_type=F32)
        return m_new

    m = jnp.full((1, width), NEG_INF, F32)
    m = lax.fori_loop(0, n_full, lambda ki, c: block(ki, c, False), m)
    lax.fori_loop(n_full, n_end, lambda ki, c: block(ki, c, True), m)
    a = acc_sc[...]
    o_ref[...] = (a[:HEAD_DIM] / a[HEAD_DIM:HEAD_DIM + 1]).astype(BF16)


def _attn_call(q_aug, k_aug, v_aug, pos0):
    bsz, _, nq, _, width = q_aug.shape
    nk, tk = k_aug.shape[2], k_aug.shape[3]
    tq = width // KV_GROUP
    assert tq & (tq - 1) == 0
    return pl.pallas_call(
        functools.partial(_attn_kernel, tq=tq, tk=tk, pos0=pos0, nk=nk),
        grid=(bsz, N_KV_HEADS, nq),
        in_specs=[pl.BlockSpec((None, None, None, ATTN_K_ROWS, width), lambda b, h, qi: (b, h, qi, 0, 0)),
                  pl.BlockSpec((None, None, nk, tk, ATTN_K_ROWS), lambda b, h, qi: (b, h, 0, 0, 0)),
                  pl.BlockSpec((None, None, nk, ATTN_V_ROWS, tk), lambda b, h, qi: (b, h, 0, 0, 0))],
        out_specs=pl.BlockSpec((None, None, None, HEAD_DIM, width), lambda b, h, qi: (b, h, qi, 0, 0)),
        out_shape=jax.ShapeDtypeStruct((bsz, N_KV_HEADS, nq, HEAD_DIM, width), BF16),
        scratch_shapes=[pltpu.VMEM((ATTN_V_ROWS, width), F32)],
        compiler_params=_params(3),
        name="forget_attention",
    )(q_aug, k_aug, v_aug)


def _split3(c):
    def top(x):
        bits = lax.bitcast_convert_type(x, jnp.uint32) & jnp.uint32(0xFFFF0000)
        return lax.bitcast_convert_type(bits, F32)
    hi = top(c)
    mid = top(c - hi)
    lo = top(c - hi - mid)
    return hi.astype(BF16), mid.astype(BF16), lo.astype(BF16)


def _attn_kv_operands(k_all, v_all, cum, tk):
    bsz, seq_k = k_all.shape[:2]
    nk = seq_k // tk
    kt = k_all.astype(BF16).transpose(0, 2, 1, 3)
    ck = jnp.stack(_split3(-cum.reshape(bsz, N_KV_HEADS, KV_GROUP, seq_k)), axis=3)
    ck = ck.reshape(bsz, N_KV_HEADS, 3 * KV_GROUP, seq_k).transpose(0, 1, 3, 2)
    ones = jnp.ones((bsz, N_KV_HEADS, seq_k, 3), BF16)
    zpad = jnp.zeros((bsz, N_KV_HEADS, seq_k, ATTN_K_ROWS - HEAD_DIM - 3 - 3 * KV_GROUP), BF16)
    k_aug = jnp.concatenate([kt, ones, ck, zpad], axis=3).reshape(bsz, N_KV_HEADS, nk, tk, ATTN_K_ROWS)
    vt = v_all.astype(BF16).transpose(0, 2, 3, 1)
    v_aug = jnp.concatenate([vt, jnp.ones((bsz, N_KV_HEADS, 1, seq_k), BF16),
                             jnp.zeros((bsz, N_KV_HEADS, ATTN_V_ROWS - HEAD_DIM - 1, seq_k), BF16)], axis=2)
    v_aug = v_aug.reshape(bsz, N_KV_HEADS, ATTN_V_ROWS, nk, tk).transpose(0, 1, 3, 2, 4)
    return k_aug, v_aug


def _attn_q_operand(q, cum_q, tq):
    bsz, seq = q.shape[:2]
    nq, width = seq // tq, KV_GROUP * tq
    lead = (bsz, N_KV_HEADS, nq)
    qt = q.reshape(bsz, nq, tq, N_KV_HEADS, KV_GROUP, HEAD_DIM).transpose(0, 3, 1, 5, 4, 2).reshape(*lead, HEAD_DIM, width)
    to_lanes = lambda c: c.reshape(bsz, N_KV_HEADS, KV_GROUP, nq, tq).transpose(0, 1, 3, 2, 4).reshape(*lead, 1, width)
    cq = [to_lanes(c) for c in _split3(cum_q)]
    sel = jnp.repeat(jnp.repeat(jnp.eye(KV_GROUP, dtype=BF16), 3, axis=0), tq, axis=1)
    sel = jnp.broadcast_to(sel, (*lead, 3 * KV_GROUP, width))
    zpad = jnp.zeros((*lead, ATTN_K_ROWS - HEAD_DIM - 3 - 3 * KV_GROUP, width), BF16)
    return jnp.concatenate([qt, *cq, sel, zpad], axis=3)


def _attn_untranspose(o_t, tq):
    bsz, _, nq = o_t.shape[:3]
    o = o_t.reshape(bsz, N_KV_HEADS, nq, HEAD_DIM, KV_GROUP, tq).transpose(0, 2, 5, 1, 4, 3)
    return o.reshape(bsz * nq * tq, D_MODEL)


def _split_mods(m, n, expand_t):
    bsz = m.shape[0]
    m = m.reshape(bsz, n, 1, D_MODEL)
    if expand_t:
        m = jnp.broadcast_to(m.transpose(1, 0, 2, 3), (n, bsz, expand_t, D_MODEL)).reshape(1, n, bsz * expand_t, D_MODEL)
    return m


def _trunk(x, ada, ada_kv, ada_final, pool_hist, kv_past, pos0, w):
    bsz, seq, _ = x.shape
    n_tok = bsz * seq
    flat_small = seq < TOKEN_TILE
    expand_t = seq if flat_small else 0
    tiles_per_mod = 1 if flat_small else seq // TOKEN_TILE
    tm = MOE_TILE if n_tok * TOP_K >= 64 * MOE_TILE else 128
    hist16 = jnp.pad(pool_hist, ((0, 0), (0, 0), (1, 0), (0, 0)))
    xs = jnp.zeros(((n_tok * TOP_K + N_EXPERTS * tm) * ROW_TILES, LANES), F32)
    new_pool = []
    xf = x.reshape(n_tok, D_MODEL)
    k4 = v4 = logf_new = k_aug = v_aug = cum_q = None
    for layer in range(DEPTH):
        lw = w["layers"][layer]
        m6 = ada[layer]
        if layer < N_A_LAYERS:
            mods = _split_mods(m6[:, :5 * D_MODEL], 5, 0)
            x1, h2, logits, hout = _pool_call(xf.reshape(bsz, seq, D_MODEL), hist16[layer], mods, lw["gm"], lw["gf"],
                                              lw["w_pool"], lw["pool_scale"], lw["wr"], lw["br"], pos0)
            new_pool.append(hout[:, 1:])
            x1 = x1.reshape(n_tok, D_MODEL)
        else:
            mods = _split_mods(m6[:, :5 * D_MODEL], 5, expand_t)
            q = _bpre_call(xf, mods[:, 0:2], lw["gm"], lw["wq"], tiles_per_mod)
            tq = min(seq, ATTN_TQ)
            o_t = _attn_call(_attn_q_operand(q.reshape(bsz, seq, D_MODEL), cum_q, tq), k_aug, v_aug, pos0)
            x1, h2, logits = _bpost_call(xf, _attn_untranspose(o_t, tq), mods[:, 2:5], lw["wo"], lw["gf"], lw["wr"],
                                         lw["br"], tiles_per_mod)
        g2 = _split_mods(m6[:, 5 * D_MODEL:], 1, expand_t)
        xf, xs = _moe(x1, h2, logits, g2, xs, lw, tm, tiles_per_mod)
        if layer == N_A_LAYERS - 1:
            mods_kv = _split_mods(ada_kv, 2, expand_t)
            k_new, v_new, lf = _kv_call(xf, mods_kv, w["g_kv"], w["wk"], w["wv"], w["wf"], w["bf"], tiles_per_mod)
            logf_new = lf[:, :N_HEADS].reshape(bsz, seq, N_HEADS)
            k4 = k_new.reshape(bsz, seq, N_KV_HEADS, HEAD_DIM)
            v4 = v_new.reshape(bsz, seq, N_KV_HEADS, HEAD_DIM)
            if kv_past is None:
                k_all, v_all, logf_all = k4, v4, logf_new
            else:
                k_all = jnp.concatenate([kv_past[0], k4], axis=1)
                v_all = jnp.concatenate([kv_past[1], v4], axis=1)
                logf_all = jnp.concatenate([kv_past[2], logf_new], axis=1)
            seq_k = k_all.shape[1]
            seq_kp = -(-seq_k // LANES) * LANES
            padk = ((0, 0), (0, seq_kp - seq_k), (0, 0), (0, 0))
            lf_t = jnp.pad(logf_all, padk[:3]).transpose(0, 2, 1).reshape(bsz * N_HEADS, seq_kp)
            cum = _cumsum_call(lf_t).reshape(bsz, N_HEADS, seq_kp)
            cum_q = cum[:, :, seq_k - seq:seq_k]
            tk = ATTN_TK if (seq >= ATTN_TQ and seq_kp % ATTN_TK == 0) else seq_kp
            k_aug, v_aug = _attn_kv_operands(jnp.pad(k_all, padk), jnp.pad(v_all, padk), cum, tk)
    mods_f = _split_mods(ada_final, 2, expand_t)
    y_out = _final_call(xf, mods_f, w["g_final"], tiles_per_mod)
    return (y_out.reshape(bsz, seq, D_MODEL), k4, v4, logf_new, jnp.stack(new_pool, axis=0))


def _pad_lanes(a, axis):
    pad = [(0, 0)] * a.ndim
    pad[axis] = (0, LANES - a.shape[axis])
    return jnp.pad(a, pad)


def kernel(x_prompt, x_sample, cache_k, cache_v, cache_logf, state_pool, c_prompt, c_sample, norm_mix_g, norm_ffn_g, w_ada, b_ada, w_pool, pool_scale, norm_kv_g, w_ada_kv, b_ada_kv, w_kv, b_forget, w_q, w_o, w_router, b_router, w_gate_up, b_gate_up, w_down, b_down, norm_final_g, w_ada_final, b_ada_final):
    bp = c_prompt.shape[0]
    c_all = jnp.concatenate([c_prompt, c_sample], axis=0)
    ada = _ada_call(c_all, w_ada, b_ada)
    w_small = jnp.stack([w_ada_kv, w_ada_final])
    b_small = jnp.stack([b_ada_kv, b_ada_final])
    ada_small = _ada_call(c_all, w_small, b_small)

    layers = []
    for layer in range(DEPTH):
        lw = dict(gm=norm_mix_g[layer][None], gf=norm_ffn_g[layer][None],
                  wr=_pad_lanes(w_router[layer], 1), br=_pad_lanes(b_router[layer][None], 1),
                  wgu=w_gate_up[layer], bgu=b_gate_up[layer][:, None, :],
                  wd=w_down[layer], bd=b_down[layer][:, None, :])
        if layer < N_A_LAYERS:
            lw.update(w_pool=w_pool[layer].astype(BF16), pool_scale=pool_scale[layer][None])
        else:
            j = layer - N_A_LAYERS
            lw.update(wq=w_q[j].astype(BF16), wo=w_o[j].astype(BF16))
        layers.append(lw)
    w = dict(layers=layers, g_kv=norm_kv_g[None], g_final=norm_final_g[None],
             wk=w_kv[:, :KV_WIDTH].astype(BF16), wv=w_kv[:, KV_WIDTH:2 * KV_WIDTH].astype(BF16),
             wf=_pad_lanes(w_kv[:, 2 * KV_WIDTH:], 1), bf=_pad_lanes(b_forget[None], 1))

    zero_hist = jnp.zeros((N_A_LAYERS, bp, POOL_HIST, D_MODEL), F32)
    y_p, k_p, v_p, lf_p, pool_p = _trunk(x_prompt, ada[:, :bp], ada_small[0, :bp], ada_small[1, :bp],
                                         zero_hist, None, 0, w)
    y_s, k_s, v_s, lf_s, pool_s = _trunk(x_sample, ada[:, bp:], ada_small[0, bp:], ada_small[1, bp:],
                                         state_pool, (cache_k, cache_v, cache_logf), cache_k.shape[1], w)
    return (y_p, y_s, k_p, v_p, lf_p, pool_p, k_s, v_s, lf_s, pool_s)
```

```python
import functools

import jax
import jax.numpy as jnp
from jax import lax
from jax.experimental import pallas as pl
from jax.experimental.pallas import tpu as pltpu

F32 = jnp.float32
BF16 = jnp.bfloat16
HIGHEST = lax.Precision.HIGHEST

D_MODEL = 1024
DEPTH = 4
N_A_LAYERS = DEPTH // 2
POOL_WINDOWS = (2, 4, 8, 16)
POOL_GROUP_DIM = D_MODEL // len(POOL_WINDOWS)
POOL_HIST = max(POOL_WINDOWS) - 1
HIST_ROWS = POOL_HIST + 1
N_HEADS = 16
HEAD_DIM = D_MODEL // N_HEADS
N_KV_HEADS = 4
KV_GROUP = N_HEADS // N_KV_HEADS
KV_WIDTH = N_KV_HEADS * HEAD_DIM
N_EXPERTS = 32
TOP_K = 4
D_FF = D_MODEL
SWIGLU_LIMIT = 7.0
SWIGLU_ALPHA = 1.702
RMS_EPS = 1e-6
NEG_INF = -1e30

LANES = 128
SUBLANES = 8
ROW_TILES = D_MODEL // LANES
TOKEN_TILE = 512
MOE_TILE = 256
FF_CHUNK = 512
DISPATCH_CHUNK = 2048
COMBINE_TILE = 256
DMA_UNROLL = 32
ATTN_TQ = 256
ATTN_TK = 256
VMEM_LIMIT = 56 * 1024 * 1024

assert ROW_TILES == SUBLANES


def _params(n_grid_dims):
    return pltpu.CompilerParams(dimension_semantics=("arbitrary",) * n_grid_dims, vmem_limit_bytes=VMEM_LIMIT)


def _norm_mod(x, gain, shift, scale):
    y = x * lax.rsqrt(jnp.mean(x * x, axis=-1, keepdims=True) + RMS_EPS) * gain
    return y * (1.0 + scale) + shift


def _store_row_tiles(ref, val):
    rows = val.shape[0]
    for j in range(ROW_TILES):
        ref[pl.ds(j, rows, stride=ROW_TILES), :] = val[:, j * LANES:(j + 1) * LANES]


def _load_row_tiles(ref, rows, first=0, stride=ROW_TILES):
    return jnp.concatenate([ref[pl.ds(first + j, rows, stride=stride), :] for j in range(ROW_TILES)], axis=1)


def _ada_kernel(c_ref, w_ref, b_ref, o_ref):
    c = c_ref[...]
    s = c * jax.nn.sigmoid(c)
    o_ref[...] = jnp.dot(s, w_ref[...], preferred_element_type=F32, precision=HIGHEST) + b_ref[...]


def _ada_call(c, w, b):
    n_layers, _, m = w.shape
    bc = c.shape[0]
    tn = 1024
    return pl.pallas_call(
        _ada_kernel,
        grid=(n_layers, m // tn),
        in_specs=[pl.BlockSpec((bc, D_MODEL), lambda l, j: (0, 0)),
                  pl.BlockSpec((None, D_MODEL, tn), lambda l, j: (l, 0, j)),
                  pl.BlockSpec((None, 1, tn), lambda l, j: (l, 0, j))],
        out_specs=pl.BlockSpec((None, bc, tn), lambda l, j: (l, 0, j)),
        out_shape=jax.ShapeDtypeStruct((n_layers, bc, m), F32),
        compiler_params=_params(2),
        name="ada",
    )(c, w, b.reshape(n_layers, 1, m))


def _tail(x1, gf_ref, sh2, sc2, wr_ref, br_ref, x1_ref, h2_ref, lg_ref):
    h2 = _norm_mod(x1, gf_ref[...], sh2, sc2)
    x1_ref[...] = x1
    _store_row_tiles(h2_ref, h2)
    lg_ref[...] = jnp.dot(h2, wr_ref[...], preferred_element_type=F32, precision=HIGHEST) + br_ref[...]


def _pool_kernel(x_ref, hist_ref, mod_ref, gm_ref, gf_ref, wp_ref, ps_ref, wr_ref, br_ref,
                 x1_ref, h2_ref, lg_ref, hout_ref, ext, *, tt, pos0):
    t = pl.program_id(1)
    x = x_ref[...]
    h = _norm_mod(x, gm_ref[...], mod_ref[0], mod_ref[1])

    @pl.when(t == 0)
    def _():
        ext[0:HIST_ROWS, :] = hist_ref[...]

    ext[HIST_ROWS:HIST_ROWS + tt, :] = h
    pos = pos0 + t * tt + lax.broadcasted_iota(jnp.int32, (tt, 1), 0)
    ys = []
    for g, w in enumerate(POOL_WINDOWS):
        lo, hi = g * POOL_GROUP_DIM, (g + 1) * POOL_GROUP_DIM
        hg = h[:, lo:hi]
        win = hg
        for j in range(1, w):
            win = win + ext[HIST_ROWS - j:HIST_ROWS - j + tt, lo:hi]
        inv_cnt = 1.0 / jnp.minimum(pos + 1, w).astype(F32)
        u = win * inv_cnt - hg
        ys.append(jnp.dot(u.astype(BF16), wp_ref[g], preferred_element_type=F32))
    y = jnp.concatenate(ys, axis=-1) * ps_ref[...]
    x1 = x + mod_ref[2] * y
    last = ext[tt:tt + HIST_ROWS, :]
    ext[0:HIST_ROWS, :] = last
    hout_ref[...] = last
    _tail(x1, gf_ref, mod_ref[3], mod_ref[4], wr_ref, br_ref, x1_ref, h2_ref, lg_ref)


def _pool_call(x, hist16, mods, gm, gf, wp, ps, wr, br, pos0):
    bsz, seq, _ = x.shape
    tt = min(seq, TOKEN_TILE)
    nt = seq // tt
    row = lambda b, t: (b, t, 0)
    flat = lambda b, t: (b * nt + t, 0)
    const2 = lambda b, t: (0, 0)
    xspec = pl.BlockSpec((None, tt, D_MODEL), row)
    specs = [xspec,
             pl.BlockSpec((None, HIST_ROWS, D_MODEL), lambda b, t: (b, 0, 0)),
             pl.BlockSpec((None, mods.shape[1], 1, D_MODEL), lambda b, t: (b, 0, 0, 0)),
             pl.BlockSpec((1, D_MODEL), const2), pl.BlockSpec((1, D_MODEL), const2),
             pl.BlockSpec(wp.shape, lambda b, t: (0, 0, 0)),
             pl.BlockSpec((1, D_MODEL), const2),
             pl.BlockSpec((D_MODEL, LANES), const2), pl.BlockSpec((1, LANES), const2)]
    return pl.pallas_call(
        functools.partial(_pool_kernel, tt=tt, pos0=pos0),
        grid=(bsz, nt),
        in_specs=specs,
        out_specs=[xspec, pl.BlockSpec((tt * ROW_TILES, LANES), flat), pl.BlockSpec((tt, LANES), flat),
                   pl.BlockSpec((None, HIST_ROWS, D_MODEL), lambda b, t: (b, 0, 0))],
        out_shape=[jax.ShapeDtypeStruct((bsz, seq, D_MODEL), F32),
                   jax.ShapeDtypeStruct((bsz * seq * ROW_TILES, LANES), F32),
                   jax.ShapeDtypeStruct((bsz * seq, LANES), F32),
                   jax.ShapeDtypeStruct((bsz, HIST_ROWS, D_MODEL), F32)],
        scratch_shapes=[pltpu.VMEM((HIST_ROWS + tt, D_MODEL), F32)],
        compiler_params=_params(2),
        name="pool_layer",
    )(x, hist16, mods, gm, gf, wp, ps, wr, br)


def _tok_specs(tt, mods, tiles_per_mod):
    nmod, rows = mods.shape[1], mods.shape[2]
    xspec = pl.BlockSpec((tt, D_MODEL), lambda i: (i, 0))
    mspec = pl.BlockSpec((None, nmod, rows, D_MODEL), lambda i: (i // tiles_per_mod, 0, 0, 0))
    return xspec, mspec


def _kv_kernel(x_ref, mod_ref, g_ref, wk_ref, wv_ref, wf_ref, bf_ref, k_out, v_out, lf_out):
    h = _norm_mod(x_ref[...], g_ref[...], mod_ref[0], mod_ref[1])
    hb = h.astype(BF16)
    k_out[...] = jnp.dot(hb, wk_ref[...], preferred_element_type=F32)
    v_out[...] = jnp.dot(hb, wv_ref[...], preferred_element_type=F32)
    z = jnp.dot(h, wf_ref[...], preferred_element_type=F32, precision=HIGHEST) + bf_ref[...]
    lf_out[...] = jax.nn.log_sigmoid(z)


def _kv_call(x, mods, g, wk, wv, wf, bf, tiles_per_mod):
    n_tok = x.shape[0]
    tt = min(n_tok, TOKEN_TILE)
    xspec, mspec = _tok_specs(tt, mods, tiles_per_mod)
    const = lambda i: (0, 0)
    kvspec = pl.BlockSpec((tt, KV_WIDTH), lambda i: (i, 0))
    return pl.pallas_call(
        _kv_kernel,
        grid=(n_tok // tt,),
        in_specs=[xspec, mspec, pl.BlockSpec((1, D_MODEL), const),
                  pl.BlockSpec((D_MODEL, KV_WIDTH), const), pl.BlockSpec((D_MODEL, KV_WIDTH), const),
                  pl.BlockSpec((D_MODEL, LANES), const), pl.BlockSpec((1, LANES), const)],
        out_specs=[kvspec, kvspec, pl.BlockSpec((tt, LANES), lambda i: (i, 0))],
        out_shape=[jax.ShapeDtypeStruct((n_tok, KV_WIDTH), F32),
                   jax.ShapeDtypeStruct((n_tok, KV_WIDTH), F32),
                   jax.ShapeDtypeStruct((n_tok, LANES), F32)],
        compiler_params=_params(1),
        name="kv_proj",
    )(x, mods, g, wk, wv, wf, bf)


def _bpre_kernel(x_ref, mod_ref, g_ref, wq_ref, q_out):
    h = _norm_mod(x_ref[...], g_ref[...], mod_ref[0], mod_ref[1])
    q = jnp.dot(h.astype(BF16), wq_ref[...], preferred_element_type=F32)
    q_out[...] = (q * (HEAD_DIM ** -0.5)).astype(BF16)


def _bpre_call(x, mods, g, wq, tiles_per_mod):
    n_tok = x.shape[0]
    tt = min(n_tok, TOKEN_TILE)
    xspec, mspec = _tok_specs(tt, mods, tiles_per_mod)
    const = lambda i: (0, 0)
    return pl.pallas_call(
        _bpre_kernel,
        grid=(n_tok // tt,),
        in_specs=[xspec, mspec, pl.BlockSpec((1, D_MODEL), const), pl.BlockSpec((D_MODEL, D_MODEL), const)],
        out_specs=xspec,
        out_shape=jax.ShapeDtypeStruct((n_tok, D_MODEL), BF16),
        compiler_params=_params(1),
        name="q_proj",
    )(x, mods, g, wq)


def _top_bits(x):
    bits = lax.bitcast_convert_type(x, jnp.uint32) & jnp.uint32(0xFFFF0000)
    return lax.bitcast_convert_type(bits, F32)


def _q_aug_kernel(x_ref, mod_ref, g_ref, wqt_ref, cum_ref, q_out, *, tq):
    tt = x_ref.shape[0]
    width = KV_GROUP * tq
    h = _norm_mod(x_ref[...], g_ref[...], mod_ref[0], mod_ref[1]).astype(BF16)
    qt = lax.dot_general(wqt_ref[...], h, (((1,), (1,)), ((), ())), preferred_element_type=F32) * (HEAD_DIM ** -0.5)
    row = lax.broadcasted_iota(jnp.int32, (16, width), 0)
    lane = lax.broadcasted_iota(jnp.int32, (16, width), 1)
    sel = jnp.zeros((16, width), F32)
    for g in range(KV_GROUP):
        own = (row >= 3 + 3 * g) & (row < 6 + 3 * g) & (lane >= g * tq) & (lane < (g + 1) * tq)
        sel = jnp.where(own, 1.0, sel)
    for kv in range(N_KV_HEADS):
        for qq in range(tt // tq):
            cols = slice(qq * tq, (qq + 1) * tq)
            for g in range(KV_GROUP):
                r0 = (kv * KV_GROUP + g) * HEAD_DIM
                q_out[kv, qq, 0:HEAD_DIM, g * tq:(g + 1) * tq] = qt[r0:r0 + HEAD_DIM, cols].astype(BF16)
            c = jnp.concatenate([cum_ref[kv * KV_GROUP + g:kv * KV_GROUP + g + 1, cols] for g in range(KV_GROUP)], axis=1)
            hi = _top_bits(c)
            mid = _top_bits(c - hi)
            lo = _top_bits(c - hi - mid)
            bias = jnp.where(row == 0, hi, jnp.where(row == 1, mid, jnp.where(row == 2, lo, sel)))
            q_out[kv, qq, HEAD_DIM:HEAD_DIM + 16, :] = bias.astype(BF16)
            q_out[kv, qq, HEAD_DIM + 16:, :] = jnp.zeros((ATTN_K_ROWS - HEAD_DIM - 16, width), BF16)


def _q_aug_call(x, mods, g, wqt, cum_q, tq, tiles_per_mod):
    n_tok = x.shape[0]
    bsz, _, seq = cum_q.shape
    tt = TOKEN_TILE
    xspec, mspec = _tok_specs(tt, mods, tiles_per_mod)
    const = lambda i: (0, 0)
    width = KV_GROUP * tq
    return pl.pallas_call(
        functools.partial(_q_aug_kernel, tq=tq),
        grid=(n_tok // tt,),
        in_specs=[xspec, mspec, pl.BlockSpec((1, D_MODEL), const), pl.BlockSpec((D_MODEL, D_MODEL), const),
                  pl.BlockSpec((None, N_HEADS, tt), lambda i: (i // tiles_per_mod, 0, i % tiles_per_mod))],
        out_specs=pl.BlockSpec((None, N_KV_HEADS, tt // tq, ATTN_K_ROWS, width),
                               lambda i: (i // tiles_per_mod, 0, i % tiles_per_mod, 0, 0)),
        out_shape=jax.ShapeDtypeStruct((bsz, N_KV_HEADS, seq // tq, ATTN_K_ROWS, width), BF16),
        compiler_params=_params(1),
        name="q_proj_aug",
    )(x, mods, g, wqt, cum_q)


def _bpost_t_kernel(x_ref, ot_ref, mod_ref, wo_ref, gf_ref, wr_ref, br_ref, x1_ref, h2_ref, lg_ref, *, tq):
    n_q = ot_ref.shape[1]
    rows = []
    for kv in range(N_KV_HEADS):
        for g in range(KV_GROUP):
            rows.append(jnp.concatenate([ot_ref[kv, qq, :, g * tq:(g + 1) * tq] for qq in range(n_q)], axis=1))
    o_t = jnp.concatenate(rows, axis=0)
    y = lax.dot_general(o_t, wo_ref[...], (((0,), (0,)), ((), ())), preferred_element_type=F32)
    x1 = x_ref[...] + mod_ref[0] * y
    _tail(x1, gf_ref, mod_ref[1], mod_ref[2], wr_ref, br_ref, x1_ref, h2_ref, lg_ref)


def _bpost_t_call(x, o_t, mods, wo, gf, wr, br, tiles_per_mod):
    n_tok = x.shape[0]
    tt = TOKEN_TILE
    width = o_t.shape[-1]
    tq = width // KV_GROUP
    xspec, mspec = _tok_specs(tt, mods, tiles_per_mod)
    const = lambda i: (0, 0)
    return pl.pallas_call(
        functools.partial(_bpost_t_kernel, tq=tq),
        grid=(n_tok // tt,),
        in_specs=[xspec, pl.BlockSpec((None, N_KV_HEADS, tt // tq, HEAD_DIM, width),
                                      lambda i: (i // tiles_per_mod, 0, i % tiles_per_mod, 0, 0)),
                  mspec, pl.BlockSpec((D_MODEL, D_MODEL), const), pl.BlockSpec((1, D_MODEL), const),
                  pl.BlockSpec((D_MODEL, LANES), const), pl.BlockSpec((1, LANES), const)],
        out_specs=[xspec, pl.BlockSpec((tt * ROW_TILES, LANES), lambda i: (i, 0)),
                   pl.BlockSpec((tt, LANES), lambda i: (i, 0))],
        out_shape=[jax.ShapeDtypeStruct((n_tok, D_MODEL), F32),
                   jax.ShapeDtypeStruct((n_tok * ROW_TILES, LANES), F32),
                   jax.ShapeDtypeStruct((n_tok, LANES), F32)],
        compiler_params=_params(1),
        name="o_proj_tail",
    )(x, o_t, mods, wo, gf, wr, br)


def _bpost_kernel(x_ref, o_ref, mod_ref, wo_ref, gf_ref, wr_ref, br_ref, x1_ref, h2_ref, lg_ref):
    y = jnp.dot(o_ref[...], wo_ref[...], preferred_element_type=F32)
    x1 = x_ref[...] + mod_ref[0] * y
    _tail(x1, gf_ref, mod_ref[1], mod_ref[2], wr_ref, br_ref, x1_ref, h2_ref, lg_ref)


def _bpost_call(x, o, mods, wo, gf, wr, br, tiles_per_mod):
    n_tok = x.shape[0]
    tt = min(n_tok, TOKEN_TILE)
    xspec, mspec = _tok_specs(tt, mods, tiles_per_mod)
    const = lambda i: (0, 0)
    return pl.pallas_call(
        _bpost_kernel,
        grid=(n_tok // tt,),
        in_specs=[xspec, xspec, mspec, pl.BlockSpec((D_MODEL, D_MODEL), const), pl.BlockSpec((1, D_MODEL), const),
                  pl.BlockSpec((D_MODEL, LANES), const), pl.BlockSpec((1, LANES), const)],
        out_specs=[xspec, pl.BlockSpec((tt * ROW_TILES, LANES), lambda i: (i, 0)),
                   pl.BlockSpec((tt, LANES), lambda i: (i, 0))],
        out_shape=[jax.ShapeDtypeStruct((n_tok, D_MODEL), F32),
                   jax.ShapeDtypeStruct((n_tok * ROW_TILES, LANES), F32),
                   jax.ShapeDtypeStruct((n_tok, LANES), F32)],
        compiler_params=_params(1),
        name="o_proj_tail",
    )(x, o, mods, wo, gf, wr, br)


def _final_kernel(x_ref, mod_ref, g_ref, y_out):
    y_out[...] = _norm_mod(x_ref[...], g_ref[...], mod_ref[0], mod_ref[1])


def _final_call(x, mods, g, tiles_per_mod):
    n_tok = x.shape[0]
    tt = min(n_tok, TOKEN_TILE)
    xspec, mspec = _tok_specs(tt, mods, tiles_per_mod)
    return pl.pallas_call(
        _final_kernel,
        grid=(n_tok // tt,),
        in_specs=[xspec, mspec, pl.BlockSpec((1, D_MODEL), lambda i: (0, 0))],
        out_specs=xspec,
        out_shape=jax.ShapeDtypeStruct((n_tok, D_MODEL), F32),
        compiler_params=_params(1),
        name="final_norm",
    )(x, mods, g)


def _route_kernel(lg_ref, idx_ref, gate_ref, rank_ref, cnt_ref, carry, *, tr):
    i = pl.program_id(0)

    @pl.when(i == 0)
    def _():
        carry[...] = jnp.zeros_like(carry)

    lane = lax.broadcasted_iota(jnp.int32, (tr, LANES), 1)
    lanef = lane.astype(F32)
    logit = jnp.where(lane < N_EXPERTS, lg_ref[...], -jnp.inf)
    r_i = lax.broadcasted_iota(jnp.int32, (tr, tr), 0)
    c_i = lax.broadcasted_iota(jnp.int32, (tr, tr), 1)
    earlier = jnp.where(c_i < r_i, 1.0, 0.0).astype(BF16)
    base = carry[...]
    idx_out = jnp.zeros((tr, LANES), F32)
    rank_out = jnp.zeros((tr, LANES), F32)
    val_out = jnp.zeros((tr, LANES), F32)
    top0 = None
    for k in range(TOP_K):
        m = jnp.max(logit, axis=1, keepdims=True)
        sel = jnp.min(jnp.where(logit == m, lanef, float(LANES)), axis=1, keepdims=True)
        onehot = lanef == sel
        ohf = jnp.where(onehot, 1.0, 0.0)
        before = jnp.dot(earlier, ohf.astype(BF16), preferred_element_type=F32)
        rank = jnp.sum(jnp.where(onehot, before + base, 0.0), axis=1, keepdims=True)
        base = base + jnp.sum(ohf, axis=0, keepdims=True)
        logit = jnp.where(onehot, -jnp.inf, logit)
        if k == 0:
            top0 = m
        idx_out = jnp.where(lane == k, sel, idx_out)
        rank_out = jnp.where(lane == k, rank, rank_out)
        val_out = jnp.where(lane == k, jnp.exp(m - top0), val_out)
    carry[...] = base
    cnt_ref[...] = base
    idx_ref[...] = idx_out
    rank_ref[...] = rank_out
    gate_ref[...] = val_out / jnp.sum(val_out, axis=1, keepdims=True)


def _route_call(logits):
    n_tok = logits.shape[0]
    tr = min(n_tok, TOKEN_TILE)
    spec = pl.BlockSpec((tr, LANES), lambda i: (i, 0))
    sds = jax.ShapeDtypeStruct((n_tok, LANES), F32)
    return pl.pallas_call(
        functools.partial(_route_kernel, tr=tr),
        grid=(n_tok // tr,),
        in_specs=[spec],
        out_specs=[spec, spec, spec, pl.BlockSpec((1, LANES), lambda i: (0, 0))],
        out_shape=[sds, sds, sds, jax.ShapeDtypeStruct((1, LANES), F32)],
        scratch_shapes=[pltpu.VMEM((1, LANES), F32)],
        compiler_params=_params(1),
        name="route",
    )(logits)


def _row_copy(src_hbm, src_row, dst_hbm, dst_row, sem):
    return pltpu.make_async_copy(src_hbm.at[pl.ds(pl.multiple_of(src_row, ROW_TILES), ROW_TILES), :],
                                 dst_hbm.at[pl.ds(pl.multiple_of(dst_row, ROW_TILES), ROW_TILES), :], sem)


def _dispatch_kernel(dst_ref, h_ref, xs_in_hbm, xs_hbm, sem, *, chunk):
    del xs_in_hbm

    def issue(g, carry):
        tok = g * (DMA_UNROLL // TOP_K)
        for u in range(DMA_UNROLL):
            _row_copy(h_ref, (tok + u // TOP_K) * ROW_TILES, xs_hbm, dst_ref[0, g * DMA_UNROLL + u], sem).start()
        return carry

    def drain(g, carry):
        for u in range(DMA_UNROLL):
            _row_copy(h_ref, 0, xs_hbm, 0, sem).wait()
        return carry

    lax.fori_loop(0, chunk // DMA_UNROLL, issue, 0)
    lax.fori_loop(0, chunk // DMA_UNROLL, drain, 0)


def _dispatch_call(h2_tiles, dst_rows, xs):
    n_asg = dst_rows.shape[0]
    chunk = min(n_asg, DISPATCH_CHUNK)
    any_spec = pl.BlockSpec(memory_space=pl.ANY)
    return pl.pallas_call(
        functools.partial(_dispatch_kernel, chunk=chunk),
        grid=(n_asg // chunk,),
        in_specs=[pl.BlockSpec((None, 1, chunk), lambda i: (i, 0, 0), memory_space=pltpu.SMEM),
                  pl.BlockSpec((chunk // TOP_K * ROW_TILES, LANES), lambda i: (i, 0)), any_spec],
        out_specs=any_spec,
        out_shape=jax.ShapeDtypeStruct(xs.shape, F32),
        scratch_shapes=[pltpu.SemaphoreType.DMA(())],
        input_output_aliases={2: 0},
        compiler_params=_params(1),
        name="moe_dispatch",
    )(dst_rows.reshape(n_asg // chunk, 1, chunk), h2_tiles, xs)


def _moe_kernel(be_ref, nv_ref, x_ref, wgu_ref, bgu_ref, wd_ref, bd_ref, o_ref, wgu_bf, wd_bf, *, tm):
    i = pl.program_id(0)
    active = i < nv_ref[0]
    fresh = jnp.logical_or(i == 0, be_ref[i] != be_ref[jnp.maximum(i - 1, 0)])

    @pl.when(jnp.logical_and(active, fresh))
    def _():
        for c in range(2 * D_FF // FF_CHUNK):
            wgu_bf[:, c * FF_CHUNK:(c + 1) * FF_CHUNK] = wgu_ref[:, c * FF_CHUNK:(c + 1) * FF_CHUNK].astype(BF16)
        for c in range(D_MODEL // FF_CHUNK):
            wd_bf[:, c * FF_CHUNK:(c + 1) * FF_CHUNK] = wd_ref[:, c * FF_CHUNK:(c + 1) * FF_CHUNK].astype(BF16)

    @pl.when(active)
    def _():
        x = _load_row_tiles(x_ref, tm).astype(BF16)
        acc = None
        for c in range(D_FF // FF_CHUNK):
            lo = c * FF_CHUNK
            glu = jnp.dot(x, wgu_bf[:, lo:lo + FF_CHUNK], preferred_element_type=F32) + bgu_ref[:, lo:lo + FF_CHUNK]
            lin = (jnp.dot(x, wgu_bf[:, D_FF + lo:D_FF + lo + FF_CHUNK], preferred_element_type=F32)
                   + bgu_ref[:, D_FF + lo:D_FF + lo + FF_CHUNK])
            glu = jnp.minimum(glu, SWIGLU_LIMIT)
            lin = jnp.clip(lin, -SWIGLU_LIMIT, SWIGLU_LIMIT)
            act = glu * jax.nn.sigmoid(SWIGLU_ALPHA * glu) * (lin + 1.0)
            part = jnp.dot(act.astype(BF16), wd_bf[lo:lo + FF_CHUNK, :], preferred_element_type=F32)
            acc = part if acc is None else acc + part
        _store_row_tiles(o_ref, acc + bd_ref[...])

    @pl.when(jnp.logical_not(active))
    def _():
        o_ref[...] = jnp.zeros_like(o_ref)


def _moe_call(xs, blk_expert, n_valid, layer, wgu, bgu, wd, bd, tm):
    n_blocks = blk_expert.shape[0]
    rspec = pl.BlockSpec((tm * ROW_TILES, LANES), lambda i, be, nv: (i, 0))
    emap = lambda i, be, nv: (layer, be[i], 0, 0)
    grid_spec = pltpu.PrefetchScalarGridSpec(
        num_scalar_prefetch=2,
        grid=(n_blocks,),
        in_specs=[rspec,
                  pl.BlockSpec((None, None, D_MODEL, 2 * D_FF), emap),
                  pl.BlockSpec((None, None, 1, 2 * D_FF), emap),
                  pl.BlockSpec((None, None, D_FF, D_MODEL), emap),
                  pl.BlockSpec((None, None, 1, D_MODEL), emap)],
        out_specs=rspec,
        scratch_shapes=[pltpu.VMEM((D_MODEL, 2 * D_FF), BF16), pltpu.VMEM((D_FF, D_MODEL), BF16)],
    )
    return pl.pallas_call(
        functools.partial(_moe_kernel, tm=tm),
        grid_spec=grid_spec,
        out_shape=jax.ShapeDtypeStruct((n_blocks * tm * ROW_TILES, LANES), F32),
        compiler_params=_params(1),
        name="moe_experts",
    )(blk_expert, n_valid, xs, wgu, bgu, wd, bd)


def _combine_kernel(cur_ref, nxt_ref, x_ref, gate_ref, g2_ref, out_hbm, x2_ref, buf, sems, *, tt, n_tiles):
    i = pl.program_id(0)
    n_copy = tt * TOP_K

    def gather(rows_ref, slot, wait):
        def body(g, carry):
            tok = g * (DMA_UNROLL // TOP_K)
            for u in range(DMA_UNROLL):
                if wait:
                    _row_copy(out_hbm, 0, buf.at[slot], 0, sems.at[slot]).wait()
                else:
                    dst = ((u % TOP_K) * tt + tok + u // TOP_K) * ROW_TILES
                    _row_copy(out_hbm, rows_ref[0, g * DMA_UNROLL + u], buf.at[slot], dst, sems.at[slot]).start()
            return carry
        lax.fori_loop(0, n_copy // DMA_UNROLL, body, 0)

    slot = i % 2

    @pl.when(i == 0)
    def _():
        gather(cur_ref, 0, False)

    @pl.when(i + 1 < n_tiles)
    def _():
        gather(nxt_ref, 1 - slot, False)

    gather(cur_ref, slot, True)
    gates = gate_ref[...]
    gate_k = [jnp.broadcast_to(gates[:, k:k + 1], (tt, LANES)) for k in range(TOP_K)]
    rows = buf.at[slot]
    for j in range(ROW_TILES):
        cols = slice(j * LANES, (j + 1) * LANES)
        y = None
        for k in range(TOP_K):
            term = gate_k[k] * rows[pl.ds(k * tt * ROW_TILES + j, tt, stride=ROW_TILES), :]
            y = term if y is None else y + term
        x2_ref[:, cols] = x_ref[:, cols] + g2_ref[0, :, cols] * y


def _combine_call(x1, gate, g2, out_tiles, pos_rows, tiles_per_mod_512):
    n_tok = x1.shape[0]
    tt = min(n_tok, COMBINE_TILE)
    n_tiles = n_tok // tt
    rows_per_mod = g2.shape[2]
    tiles_per_mod = tiles_per_mod_512 * (TOKEN_TILE // tt) if rows_per_mod == 1 else 1
    if rows_per_mod == 1:
        g2spec = pl.BlockSpec((None, 1, 1, D_MODEL), lambda i: (i // tiles_per_mod, 0, 0, 0))
    else:
        g2spec = pl.BlockSpec((None, 1, tt, D_MODEL), lambda i: (0, 0, i, 0))
    pos3 = pos_rows.reshape(n_tiles, 1, tt * TOP_K)
    pspec = lambda f: pl.BlockSpec((None, 1, tt * TOP_K), f, memory_space=pltpu.SMEM)
    return pl.pallas_call(
        functools.partial(_combine_kernel, tt=tt, n_tiles=n_tiles),
        grid=(n_tiles,),
        in_specs=[pspec(lambda i: (i, 0, 0)), pspec(lambda i: (jnp.minimum(i + 1, n_tiles - 1), 0, 0)),
                  pl.BlockSpec((tt, D_MODEL), lambda i: (i, 0)), pl.BlockSpec((tt, LANES), lambda i: (i, 0)),
                  g2spec, pl.BlockSpec(memory_space=pl.ANY)],
        out_specs=pl.BlockSpec((tt, D_MODEL), lambda i: (i, 0)),
        out_shape=jax.ShapeDtypeStruct((n_tok, D_MODEL), F32),
        scratch_shapes=[pltpu.VMEM((2, tt * TOP_K * ROW_TILES, LANES), F32), pltpu.SemaphoreType.DMA((2,))],
        compiler_params=_params(1),
        name="moe_combine",
    )(pos3, pos3, x1, gate, g2, out_tiles)


def _moe(x1, h2_tiles, logits, g2, xs, layer, experts, tm, tiles_per_mod):
    n_tok = x1.shape[0]
    idx_f, gate_f, rank_f, cnt_f = _route_call(logits)
    idx = idx_f[:, :TOP_K].astype(jnp.int32)
    rank = rank_f[:, :TOP_K].astype(jnp.int32)
    counts = cnt_f[0, :N_EXPERTS].astype(jnp.int32)
    padded = (counts + tm - 1) // tm * tm
    pad_end = jnp.cumsum(padded)
    pad_start = pad_end - padded
    pos_rows = ((pad_start[idx] + rank) * ROW_TILES).reshape(-1)
    n_blocks = n_tok * TOP_K // tm + N_EXPERTS
    blk_start = jnp.arange(n_blocks, dtype=jnp.int32) * tm
    blk_expert = jnp.minimum(jnp.sum(pad_end[None, :] <= blk_start[:, None], axis=1), N_EXPERTS - 1).astype(jnp.int32)
    n_valid = (pad_end[-1] // tm).astype(jnp.int32).reshape(1)
    xs = _dispatch_call(h2_tiles, pos_rows, xs)
    out = _moe_call(xs, blk_expert, n_valid, layer, *experts, tm)
    return _combine_call(x1, gate_f, g2, out, pos_rows, tiles_per_mod), xs


def _cumsum_kernel(x_ref, o_ref, carry, *, tc):
    j = pl.program_id(1)

    @pl.when(j == 0)
    def _():
        carry[...] = jnp.zeros_like(carry)

    r_i = lax.broadcasted_iota(jnp.int32, (tc, tc), 0)
    c_i = lax.broadcasted_iota(jnp.int32, (tc, tc), 1)
    upto = jnp.where(r_i <= c_i, 1.0, 0.0)
    cs = jnp.dot(x_ref[...], upto, preferred_element_type=F32, precision=HIGHEST) + carry[...]
    o_ref[...] = cs
    carry[...] = cs[:, tc - 1:tc]


def _cumsum_call(x):
    rows, tk = x.shape
    tc = LANES
    tr = min(rows, LANES)
    spec = pl.BlockSpec((tr, tc), lambda r, j: (r, j))
    return pl.pallas_call(
        functools.partial(_cumsum_kernel, tc=tc),
        grid=(rows // tr, tk // tc),
        in_specs=[spec],
        out_specs=spec,
        out_shape=jax.ShapeDtypeStruct(x.shape, F32),
        scratch_shapes=[pltpu.VMEM((tr, 1), F32)],
        compiler_params=_params(2),
        name="logf_cumsum",
    )(x)


ATTN_K_ROWS = LANES
ATTN_V_ROWS = HEAD_DIM + 16


def _attn_kernel(q_ref, k_ref, v_ref, o_ref, acc_sc, *, tq, tk, pos0, nk):
    qi = pl.program_id(2)
    q_first = pos0 + qi * tq
    n_full = jnp.minimum(nk, (q_first + 1) // tk)
    n_end = jnp.minimum(nk, (q_first + tq - 1) // tk + 1)
    acc_sc[...] = jnp.zeros_like(acc_sc)
    width = KV_GROUP * tq

    def block(ki, m, masked):
        s = jnp.dot(k_ref[ki], q_ref[...], preferred_element_type=F32)
        if masked:
            key = ki * tk + lax.broadcasted_iota(jnp.int32, (tk, width), 0)
            qry = q_first + (lax.broadcasted_iota(jnp.int32, (tk, width), 1) & (tq - 1))
            s = jnp.where(key <= qry, s, NEG_INF)
        m_new = jnp.maximum(m, jnp.max(s, axis=0, keepdims=True))
        alpha = jnp.exp(m - m_new)
        p = jnp.exp(s - m_new).astype(BF16)
        acc_sc[...] = alpha * acc_sc[...] + jnp.dot(v_ref[ki], p, preferred_element_type=F32)
        return m_new

    m = jnp.full((1, width), NEG_INF, F32)
    m = lax.fori_loop(0, n_full, lambda ki, c: block(ki, c, False), m)
    lax.fori_loop(n_full, n_end, lambda ki, c: block(ki, c, True), m)
    a = acc_sc[...]
    o_ref[...] = (a[:HEAD_DIM] / a[HEAD_DIM:HEAD_DIM + 1]).astype(BF16)


def _attn_call(q_aug, k_aug, v_aug, pos0):
    bsz, _, nq, _, width = q_aug.shape
    nk, tk = k_aug.shape[2], k_aug.shape[3]
    tq = width // KV_GROUP
    assert tq & (tq - 1) == 0
    return pl.pallas_call(
        functools.partial(_attn_kernel, tq=tq, tk=tk, pos0=pos0, nk=nk),
        grid=(bsz, N_KV_HEADS, nq),
        in_specs=[pl.BlockSpec((None, None, None, ATTN_K_ROWS, width), lambda b, h, qi: (b, h, qi, 0, 0)),
                  pl.BlockSpec((None, None, nk, tk, ATTN_K_ROWS), lambda b, h, qi: (b, h, 0, 0, 0)),
                  pl.BlockSpec((None, None, nk, ATTN_V_ROWS, tk), lambda b, h, qi: (b, h, 0, 0, 0))],
        out_specs=pl.BlockSpec((None, None, None, HEAD_DIM, width), lambda b, h, qi: (b, h, qi, 0, 0)),
        out_shape=jax.ShapeDtypeStruct((bsz, N_KV_HEADS, nq, HEAD_DIM, width), BF16),
        scratch_shapes=[pltpu.VMEM((ATTN_V_ROWS, width), F32)],
        compiler_params=_params(3),
        name="forget_attention",
    )(q_aug, k_aug, v_aug)


def _split3(c):
    def top(x):
        bits = lax.bitcast_convert_type(x, jnp.uint32) & jnp.uint32(0xFFFF0000)
        return lax.bitcast_convert_type(bits, F32)
    hi = top(c)
    mid = top(c - hi)
    lo = top(c - hi - mid)
    return hi.astype(BF16), mid.astype(BF16), lo.astype(BF16)


def _attn_kv_operands(k_all, v_all, cum, tk):
    bsz, seq_k = k_all.shape[:2]
    nk = seq_k // tk
    kt = k_all.astype(BF16).transpose(0, 2, 1, 3)
    ck = jnp.stack(_split3(-cum.reshape(bsz, N_KV_HEADS, KV_GROUP, seq_k)), axis=3)
    ck = ck.reshape(bsz, N_KV_HEADS, 3 * KV_GROUP, seq_k).transpose(0, 1, 3, 2)
    ones = jnp.ones((bsz, N_KV_HEADS, seq_k, 3), BF16)
    zpad = jnp.zeros((bsz, N_KV_HEADS, seq_k, ATTN_K_ROWS - HEAD_DIM - 3 - 3 * KV_GROUP), BF16)
    k_aug = jnp.concatenate([kt, ones, ck, zpad], axis=3).reshape(bsz, N_KV_HEADS, nk, tk, ATTN_K_ROWS)
    vt = v_all.astype(BF16).transpose(0, 2, 3, 1)
    v_aug = jnp.concatenate([vt, jnp.ones((bsz, N_KV_HEADS, 1, seq_k), BF16),
                             jnp.zeros((bsz, N_KV_HEADS, ATTN_V_ROWS - HEAD_DIM - 1, seq_k), BF16)], axis=2)
    v_aug = v_aug.reshape(bsz, N_KV_HEADS, ATTN_V_ROWS, nk, tk).transpose(0, 1, 3, 2, 4)
    return k_aug, v_aug


def _attn_q_operand(q, cum_q, tq):
    bsz, seq = q.shape[:2]
    nq, width = seq // tq, KV_GROUP * tq
    lead = (bsz, N_KV_HEADS, nq)
    qt = q.reshape(bsz, nq, tq, N_KV_HEADS, KV_GROUP, HEAD_DIM).transpose(0, 3, 1, 5, 4, 2).reshape(*lead, HEAD_DIM, width)
    to_lanes = lambda c: c.reshape(bsz, N_KV_HEADS, KV_GROUP, nq, tq).transpose(0, 1, 3, 2, 4).reshape(*lead, 1, width)
    cq = [to_lanes(c) for c in _split3(cum_q)]
    sel = jnp.repeat(jnp.repeat(jnp.eye(KV_GROUP, dtype=BF16), 3, axis=0), tq, axis=1)
    sel = jnp.broadcast_to(sel, (*lead, 3 * KV_GROUP, width))
    zpad = jnp.zeros((*lead, ATTN_K_ROWS - HEAD_DIM - 3 - 3 * KV_GROUP, width), BF16)
    return jnp.concatenate([qt, *cq, sel, zpad], axis=3)


def _attn_untranspose(o_t, tq):
    bsz, _, nq = o_t.shape[:3]
    o = o_t.reshape(bsz, N_KV_HEADS, nq, HEAD_DIM, KV_GROUP, tq).transpose(0, 2, 5, 1, 4, 3)
    return o.reshape(bsz * nq * tq, D_MODEL)


def _split_mods(m, n, expand_t):
    bsz = m.shape[0]
    m = m.reshape(bsz, n, 1, D_MODEL)
    if expand_t:
        m = jnp.broadcast_to(m.transpose(1, 0, 2, 3), (n, bsz, expand_t, D_MODEL)).reshape(1, n, bsz * expand_t, D_MODEL)
    return m


def _trunk(x, ada, ada_kv, ada_final, pool_hist, kv_past, pos0, w):
    bsz, seq, _ = x.shape
    n_tok = bsz * seq
    flat_small = seq < TOKEN_TILE
    expand_t = seq if flat_small else 0
    tiles_per_mod = 1 if flat_small else seq // TOKEN_TILE
    tm = MOE_TILE if n_tok * TOP_K >= 64 * MOE_TILE else 128
    hist16 = jnp.pad(pool_hist, ((0, 0), (0, 0), (1, 0), (0, 0)))
    xs = jnp.zeros(((n_tok * TOP_K + N_EXPERTS * tm) * ROW_TILES, LANES), F32)
    new_pool = []
    xf = x.reshape(n_tok, D_MODEL)
    k4 = v4 = logf_new = k_aug = v_aug = cum_q = None
    for layer in range(DEPTH):
        lw = w["layers"][layer]
        m6 = ada[layer]
        if layer < N_A_LAYERS:
            mods = _split_mods(m6[:, :5 * D_MODEL], 5, 0)
            x1, h2, logits, hout = _pool_call(xf.reshape(bsz, seq, D_MODEL), hist16[layer], mods, lw["gm"], lw["gf"],
                                              lw["w_pool"], lw["pool_scale"], lw["wr"], lw["br"], pos0)
            new_pool.append(hout[:, 1:])
            x1 = x1.reshape(n_tok, D_MODEL)
        else:
            mods = _split_mods(m6[:, :5 * D_MODEL], 5, expand_t)
            tq = min(seq, ATTN_TQ)
            if flat_small:
                q = _bpre_call(xf, mods[:, 0:2], lw["gm"], lw["wq"], tiles_per_mod)
                o_t = _attn_call(_attn_q_operand(q.reshape(bsz, seq, D_MODEL), cum_q, tq), k_aug, v_aug, pos0)
                x1, h2, logits = _bpost_call(xf, _attn_untranspose(o_t, tq), mods[:, 2:5], lw["wo"], lw["gf"],
                                             lw["wr"], lw["br"], tiles_per_mod)
            else:
                q_aug = _q_aug_call(xf, mods[:, 0:2], lw["gm"], lw["wqt"], cum_q, tq, tiles_per_mod)
                o_t = _attn_call(q_aug, k_aug, v_aug, pos0)
                x1, h2, logits = _bpost_t_call(xf, o_t, mods[:, 2:5], lw["wo"], lw["gf"], lw["wr"], lw["br"],
                                               tiles_per_mod)
        g2 = _split_mods(m6[:, 5 * D_MODEL:], 1, expand_t)
        xf, xs = _moe(x1, h2, logits, g2, xs, layer, w["experts"], tm, tiles_per_mod)
        if layer == N_A_LAYERS - 1:
            mods_kv = _split_mods(ada_kv, 2, expand_t)
            k_new, v_new, lf = _kv_call(xf, mods_kv, w["g_kv"], w["wk"], w["wv"], w["wf"], w["bf"], tiles_per_mod)
            logf_new = lf[:, :N_HEADS].reshape(bsz, seq, N_HEADS)
            k4 = k_new.reshape(bsz, seq, N_KV_HEADS, HEAD_DIM)
            v4 = v_new.reshape(bsz, seq, N_KV_HEADS, HEAD_DIM)
            if kv_past is None:
                k_all, v_all, logf_all = k4, v4, logf_new
            else:
                k_all = jnp.concatenate([kv_past[0], k4], axis=1)
                v_all = jnp.concatenate([kv_past[1], v4], axis=1)
                logf_all = jnp.concatenate([kv_past[2], logf_new], axis=1)
            seq_k = k_all.shape[1]
            seq_kp = -(-seq_k // LANES) * LANES
            padk = ((0, 0), (0, seq_kp - seq_k), (0, 0), (0, 0))
            lf_t = jnp.pad(logf_all, padk[:3]).transpose(0, 2, 1).reshape(bsz * N_HEADS, seq_kp)
            cum = _cumsum_call(lf_t).reshape(bsz, N_HEADS, seq_kp)
            cum_q = cum[:, :, seq_k - seq:seq_k]
            tk = ATTN_TK if (seq >= ATTN_TQ and seq_kp % ATTN_TK == 0) else seq_kp
            k_aug, v_aug = _attn_kv_operands(jnp.pad(k_all, padk), jnp.pad(v_all, padk), cum, tk)
    mods_f = _split_mods(ada_final, 2, expand_t)
    y_out = _final_call(xf, mods_f, w["g_final"], tiles_per_mod)
    return (y_out.reshape(bsz, seq, D_MODEL), k4, v4, logf_new, jnp.stack(new_pool, axis=0))


def _pad_lanes(a, axis):
    pad = [(0, 0)] * a.ndim
    pad[axis] = (0, LANES - a.shape[axis])
    return jnp.pad(a, pad)


def kernel(x_prompt, x_sample, cache_k, cache_v, cache_logf, state_pool, c_prompt, c_sample, norm_mix_g, norm_ffn_g, w_ada, b_ada, w_pool, pool_scale, norm_kv_g, w_ada_kv, b_ada_kv, w_kv, b_forget, w_q, w_o, w_router, b_router, w_gate_up, b_gate_up, w_down, b_down, norm_final_g, w_ada_final, b_ada_final):
    bp = c_prompt.shape[0]
    c_all = jnp.concatenate([c_prompt, c_sample], axis=0)
    ada = _ada_call(c_all, w_ada, b_ada)
    w_small = jnp.stack([w_ada_kv, w_ada_final])
    b_small = jnp.stack([b_ada_kv, b_ada_final])
    ada_small = _ada_call(c_all, w_small, b_small)

    layers = []
    for layer in range(DEPTH):
        lw = dict(gm=norm_mix_g[layer][None], gf=norm_ffn_g[layer][None],
                  wr=_pad_lanes(w_router[layer], 1), br=_pad_lanes(b_router[layer][None], 1),
                  )
        if layer < N_A_LAYERS:
            lw.update(w_pool=w_pool[layer].astype(BF16), pool_scale=pool_scale[layer][None])
        else:
            j = layer - N_A_LAYERS
            lw.update(wq=w_q[j].astype(BF16), wqt=w_q[j].T.astype(BF16), wo=w_o[j].astype(BF16))
        layers.append(lw)
    w = dict(layers=layers, g_kv=norm_kv_g[None], g_final=norm_final_g[None],
             wk=w_kv[:, :KV_WIDTH].astype(BF16), wv=w_kv[:, KV_WIDTH:2 * KV_WIDTH].astype(BF16),
             wf=_pad_lanes(w_kv[:, 2 * KV_WIDTH:], 1), bf=_pad_lanes(b_forget[None], 1),
             experts=(w_gate_up, b_gate_up[:, :, None, :], w_down, b_down[:, :, None, :]))

    zero_hist = jnp.zeros((N_A_LAYERS, bp, POOL_HIST, D_MODEL), F32)
    y_p, k_p, v_p, lf_p, pool_p = _trunk(x_prompt, ada[:, :bp], ada_small[0, :bp], ada_small[1, :bp],
                                         zero_hist, None, 0, w)
    y_s, k_s, v_s, lf_s, pool_s = _trunk(x_sample, ada[:, bp:], ada_small[0, bp:], ada_small[1, bp:],
                                         state_pool, (cache_k, cache_v, cache_logf), cache_k.shape[1], w)
    return (y_p, y_s, k_p, v_p, lf_p, pool_p, k_s, v_s, lf_s, pool_s)
```

```python
import functools

import jax
import jax.numpy as jnp
from jax import lax
from jax.experimental import pallas as pl
from jax.experimental.pallas import tpu as pltpu

F32 = jnp.float32
BF16 = jnp.bfloat16
HIGHEST = lax.Precision.HIGHEST

D_MODEL = 1024
DEPTH = 4
N_A_LAYERS = DEPTH // 2
POOL_WINDOWS = (2, 4, 8, 16)
POOL_GROUP_DIM = D_MODEL // len(POOL_WINDOWS)
POOL_HIST = max(POOL_WINDOWS) - 1
HIST_ROWS = POOL_HIST + 1
N_HEADS = 16
HEAD_DIM = D_MODEL // N_HEADS
N_KV_HEADS = 4
KV_GROUP = N_HEADS // N_KV_HEADS
KV_WIDTH = N_KV_HEADS * HEAD_DIM
N_EXPERTS = 32
TOP_K = 4
D_FF = D_MODEL
SWIGLU_LIMIT = 7.0
SWIGLU_ALPHA = 1.702
RMS_EPS = 1e-6
NEG_INF = -1e30

LANES = 128
SUBLANES = 8
ROW_TILES = D_MODEL // LANES
TOKEN_TILE = 512
MOE_TILE = 512
FF_CHUNK = 512
DISPATCH_CHUNK = 2048
COMBINE_TILE = 256
DMA_UNROLL = 32
ATTN_TQ = 256
ATTN_TK = 256
VMEM_LIMIT = 56 * 1024 * 1024

assert ROW_TILES == SUBLANES


def _params(n_grid_dims):
    return pltpu.CompilerParams(dimension_semantics=("arbitrary",) * n_grid_dims, vmem_limit_bytes=VMEM_LIMIT)


def _norm_mod(x, gain, shift, scale):
    y = x * lax.rsqrt(jnp.mean(x * x, axis=-1, keepdims=True) + RMS_EPS) * gain
    return y * (1.0 + scale) + shift


def _store_row_tiles(ref, val):
    rows = val.shape[0]
    for j in range(ROW_TILES):
        ref[pl.ds(j, rows, stride=ROW_TILES), :] = val[:, j * LANES:(j + 1) * LANES]


def _load_row_tiles(ref, rows, first=0, stride=ROW_TILES):
    return jnp.concatenate([ref[pl.ds(first + j, rows, stride=stride), :] for j in range(ROW_TILES)], axis=1)


def _ada_kernel(c_ref, w_ref, b_ref, o_ref):
    c = c_ref[...]
    s = c * jax.nn.sigmoid(c)
    o_ref[...] = jnp.dot(s, w_ref[...], preferred_element_type=F32, precision=HIGHEST) + b_ref[...]


def _ada_call(c, w, b):
    n_layers, _, m = w.shape
    bc = c.shape[0]
    tn = 1024
    return pl.pallas_call(
        _ada_kernel,
        grid=(n_layers, m // tn),
        in_specs=[pl.BlockSpec((bc, D_MODEL), lambda l, j: (0, 0)),
                  pl.BlockSpec((None, D_MODEL, tn), lambda l, j: (l, 0, j)),
                  pl.BlockSpec((None, 1, tn), lambda l, j: (l, 0, j))],
        out_specs=pl.BlockSpec((None, bc, tn), lambda l, j: (l, 0, j)),
        out_shape=jax.ShapeDtypeStruct((n_layers, bc, m), F32),
        compiler_params=_params(2),
        name="ada",
    )(c, w, b.reshape(n_layers, 1, m))


def _top_bits(x):
    bits = lax.bitcast_convert_type(x, jnp.uint32) & jnp.uint32(0xFFFF0000)
    return lax.bitcast_convert_type(bits, F32)


def _tail(x1, gf_ref, sh2, sc2, wr_ref, br_ref, x1_ref, h2_ref, lg_ref):
    h2 = _norm_mod(x1, gf_ref[...], sh2, sc2)
    x1_ref[...] = x1
    _store_row_tiles(h2_ref, h2)
    hi = _top_bits(h2)
    lo = (h2 - hi).astype(BF16)
    hi = hi.astype(BF16)
    lg_ref[...] = (jnp.dot(hi, wr_ref[0], preferred_element_type=F32) + jnp.dot(hi, wr_ref[1], preferred_element_type=F32)
                   + jnp.dot(lo, wr_ref[0], preferred_element_type=F32) + br_ref[...])


def _pool_kernel(x_ref, hist_ref, mod_ref, gm_ref, gf_ref, wp_ref, ps_ref, wr_ref, br_ref,
                 x1_ref, h2_ref, lg_ref, hout_ref, ext, *, tt, pos0):
    t = pl.program_id(1)
    x = x_ref[...]
    h = _norm_mod(x, gm_ref[...], mod_ref[0], mod_ref[1])

    @pl.when(t == 0)
    def _():
        ext[0:HIST_ROWS, :] = hist_ref[...]

    ext[HIST_ROWS:HIST_ROWS + tt, :] = h
    pos = pos0 + t * tt + lax.broadcasted_iota(jnp.int32, (tt, 1), 0)
    ys = []
    for g, w in enumerate(POOL_WINDOWS):
        lo, hi = g * POOL_GROUP_DIM, (g + 1) * POOL_GROUP_DIM
        hg = h[:, lo:hi]
        win = hg
        for j in range(1, w):
            win = win + ext[HIST_ROWS - j:HIST_ROWS - j + tt, lo:hi]
        inv_cnt = 1.0 / jnp.minimum(pos + 1, w).astype(F32)
        u = win * inv_cnt - hg
        ys.append(jnp.dot(u.astype(BF16), wp_ref[g], preferred_element_type=F32))
    y = jnp.concatenate(ys, axis=-1) * ps_ref[...]
    x1 = x + mod_ref[2] * y
    last = ext[tt:tt + HIST_ROWS, :]
    ext[0:HIST_ROWS, :] = last
    hout_ref[...] = last
    _tail(x1, gf_ref, mod_ref[3], mod_ref[4], wr_ref, br_ref, x1_ref, h2_ref, lg_ref)


def _pool_call(x, hist16, mods, gm, gf, wp, ps, wr, br, pos0):
    bsz, seq, _ = x.shape
    tt = min(seq, TOKEN_TILE)
    nt = seq // tt
    row = lambda b, t: (b, t, 0)
    flat = lambda b, t: (b * nt + t, 0)
    const2 = lambda b, t: (0, 0)
    xspec = pl.BlockSpec((None, tt, D_MODEL), row)
    specs = [xspec,
             pl.BlockSpec((None, HIST_ROWS, D_MODEL), lambda b, t: (b, 0, 0)),
             pl.BlockSpec((None, mods.shape[1], 1, D_MODEL), lambda b, t: (b, 0, 0, 0)),
             pl.BlockSpec((1, D_MODEL), const2), pl.BlockSpec((1, D_MODEL), const2),
             pl.BlockSpec(wp.shape, lambda b, t: (0, 0, 0)),
             pl.BlockSpec((1, D_MODEL), const2),
             pl.BlockSpec((2, D_MODEL, LANES), lambda b, t: (0, 0, 0)), pl.BlockSpec((1, LANES), const2)]
    return pl.pallas_call(
        functools.partial(_pool_kernel, tt=tt, pos0=pos0),
        grid=(bsz, nt),
        in_specs=specs,
        out_specs=[xspec, pl.BlockSpec((tt * ROW_TILES, LANES), flat), pl.BlockSpec((tt, LANES), flat),
                   pl.BlockSpec((None, HIST_ROWS, D_MODEL), lambda b, t: (b, 0, 0))],
        out_shape=[jax.ShapeDtypeStruct((bsz, seq, D_MODEL), F32),
                   jax.ShapeDtypeStruct((bsz * seq * ROW_TILES, LANES), F32),
                   jax.ShapeDtypeStruct((bsz * seq, LANES), F32),
                   jax.ShapeDtypeStruct((bsz, HIST_ROWS, D_MODEL), F32)],
        scratch_shapes=[pltpu.VMEM((HIST_ROWS + tt, D_MODEL), F32)],
        compiler_params=_params(2),
        name="pool_layer",
    )(x, hist16, mods, gm, gf, wp, ps, wr, br)


def _tok_specs(tt, mods, tiles_per_mod):
    nmod, rows = mods.shape[1], mods.shape[2]
    xspec = pl.BlockSpec((tt, D_MODEL), lambda i: (i, 0))
    mspec = pl.BlockSpec((None, nmod, rows, D_MODEL), lambda i: (i // tiles_per_mod, 0, 0, 0))
    return xspec, mspec


def _kv_kernel(x_ref, mod_ref, g_ref, wk_ref, wv_ref, wf_ref, bf_ref, k_out, v_out, lf_out):
    h = _norm_mod(x_ref[...], g_ref[...], mod_ref[0], mod_ref[1])
    hb = h.astype(BF16)
    k_out[...] = jnp.dot(hb, wk_ref[...], preferred_element_type=F32)
    v_out[...] = jnp.dot(hb, wv_ref[...], preferred_element_type=F32)
    z = jnp.dot(h, wf_ref[...], preferred_element_type=F32, precision=HIGHEST) + bf_ref[...]
    lf_out[...] = jax.nn.log_sigmoid(z)


def _kv_call(x, mods, g, wk, wv, wf, bf, tiles_per_mod):
    n_tok = x.shape[0]
    tt = min(n_tok, TOKEN_TILE)
    xspec, mspec = _tok_specs(tt, mods, tiles_per_mod)
    const = lambda i: (0, 0)
    kvspec = pl.BlockSpec((tt, KV_WIDTH), lambda i: (i, 0))
    return pl.pallas_call(
        _kv_kernel,
        grid=(n_tok // tt,),
        in_specs=[xspec, mspec, pl.BlockSpec((1, D_MODEL), const),
                  pl.BlockSpec((D_MODEL, KV_WIDTH), const), pl.BlockSpec((D_MODEL, KV_WIDTH), const),
                  pl.BlockSpec((D_MODEL, LANES), const), pl.BlockSpec((1, LANES), const)],
        out_specs=[kvspec, kvspec, pl.BlockSpec((tt, LANES), lambda i: (i, 0))],
        out_shape=[jax.ShapeDtypeStruct((n_tok, KV_WIDTH), F32),
                   jax.ShapeDtypeStruct((n_tok, KV_WIDTH), F32),
                   jax.ShapeDtypeStruct((n_tok, LANES), F32)],
        compiler_params=_params(1),
        name="kv_proj",
    )(x, mods, g, wk, wv, wf, bf)


def _bpre_kernel(x_ref, mod_ref, g_ref, wq_ref, q_out):
    h = _norm_mod(x_ref[...], g_ref[...], mod_ref[0], mod_ref[1])
    q = jnp.dot(h.astype(BF16), wq_ref[...], preferred_element_type=F32)
    q_out[...] = (q * (HEAD_DIM ** -0.5)).astype(BF16)


def _bpre_call(x, mods, g, wq, tiles_per_mod):
    n_tok = x.shape[0]
    tt = min(n_tok, TOKEN_TILE)
    xspec, mspec = _tok_specs(tt, mods, tiles_per_mod)
    const = lambda i: (0, 0)
    return pl.pallas_call(
        _bpre_kernel,
        grid=(n_tok // tt,),
        in_specs=[xspec, mspec, pl.BlockSpec((1, D_MODEL), const), pl.BlockSpec((D_MODEL, D_MODEL), const)],
        out_specs=xspec,
        out_shape=jax.ShapeDtypeStruct((n_tok, D_MODEL), BF16),
        compiler_params=_params(1),
        name="q_proj",
    )(x, mods, g, wq)


def _q_aug_kernel(x_ref, mod_ref, g_ref, wqt_ref, cum_ref, q_out, *, tq):
    tt = x_ref.shape[0]
    width = KV_GROUP * tq
    h = _norm_mod(x_ref[...], g_ref[...], mod_ref[0], mod_ref[1]).astype(BF16)
    qt = lax.dot_general(wqt_ref[...], h, (((1,), (1,)), ((), ())), preferred_element_type=F32) * (HEAD_DIM ** -0.5)
    row = lax.broadcasted_iota(jnp.int32, (16, width), 0)
    lane = lax.broadcasted_iota(jnp.int32, (16, width), 1)
    sel = jnp.zeros((16, width), F32)
    for g in range(KV_GROUP):
        own = (row >= 3 + 3 * g) & (row < 6 + 3 * g) & (lane >= g * tq) & (lane < (g + 1) * tq)
        sel = jnp.where(own, 1.0, sel)
    for kv in range(N_KV_HEADS):
        for qq in range(tt // tq):
            cols = slice(qq * tq, (qq + 1) * tq)
            for g in range(KV_GROUP):
                r0 = (kv * KV_GROUP + g) * HEAD_DIM
                q_out[kv, qq, 0:HEAD_DIM, g * tq:(g + 1) * tq] = qt[r0:r0 + HEAD_DIM, cols].astype(BF16)
            c = jnp.concatenate([cum_ref[kv * KV_GROUP + g:kv * KV_GROUP + g + 1, cols] for g in range(KV_GROUP)], axis=1)
            hi = _top_bits(c)
            mid = _top_bits(c - hi)
            lo = _top_bits(c - hi - mid)
            bias = jnp.where(row == 0, hi, jnp.where(row == 1, mid, jnp.where(row == 2, lo, sel)))
            q_out[kv, qq, HEAD_DIM:HEAD_DIM + 16, :] = bias.astype(BF16)
            q_out[kv, qq, HEAD_DIM + 16:, :] = jnp.zeros((ATTN_K_ROWS - HEAD_DIM - 16, width), BF16)


def _q_aug_call(x, mods, g, wqt, cum_q, tq, tiles_per_mod):
    n_tok = x.shape[0]
    bsz, _, seq = cum_q.shape
    tt = TOKEN_TILE
    xspec, mspec = _tok_specs(tt, mods, tiles_per_mod)
    const = lambda i: (0, 0)
    width = KV_GROUP * tq
    return pl.pallas_call(
        functools.partial(_q_aug_kernel, tq=tq),
        grid=(n_tok // tt,),
        in_specs=[xspec, mspec, pl.BlockSpec((1, D_MODEL), const), pl.BlockSpec((D_MODEL, D_MODEL), const),
                  pl.BlockSpec((None, N_HEADS, tt), lambda i: (i // tiles_per_mod, 0, i % tiles_per_mod))],
        out_specs=pl.BlockSpec((None, N_KV_HEADS, tt // tq, ATTN_K_ROWS, width),
                               lambda i: (i // tiles_per_mod, 0, i % tiles_per_mod, 0, 0)),
        out_shape=jax.ShapeDtypeStruct((bsz, N_KV_HEADS, seq // tq, ATTN_K_ROWS, width), BF16),
        compiler_params=_params(1),
        name="q_proj_aug",
    )(x, mods, g, wqt, cum_q)


def _bpost_t_kernel(x_ref, ot_ref, mod_ref, wo_ref, gf_ref, wr_ref, br_ref, x1_ref, h2_ref, lg_ref, *, tq):
    n_q = ot_ref.shape[1]
    rows = []
    for kv in range(N_KV_HEADS):
        for g in range(KV_GROUP):
            rows.append(jnp.concatenate([ot_ref[kv, qq, :, g * tq:(g + 1) * tq] for qq in range(n_q)], axis=1))
    o_t = jnp.concatenate(rows, axis=0)
    y = lax.dot_general(o_t, wo_ref[...], (((0,), (0,)), ((), ())), preferred_element_type=F32)
    x1 = x_ref[...] + mod_ref[0] * y
    _tail(x1, gf_ref, mod_ref[1], mod_ref[2], wr_ref, br_ref, x1_ref, h2_ref, lg_ref)


def _bpost_t_call(x, o_t, mods, wo, gf, wr, br, tiles_per_mod):
    n_tok = x.shape[0]
    tt = TOKEN_TILE
    width = o_t.shape[-1]
    tq = width // KV_GROUP
    xspec, mspec = _tok_specs(tt, mods, tiles_per_mod)
    const = lambda i: (0, 0)
    return pl.pallas_call(
        functools.partial(_bpost_t_kernel, tq=tq),
        grid=(n_tok // tt,),
        in_specs=[xspec, pl.BlockSpec((None, N_KV_HEADS, tt // tq, HEAD_DIM, width),
                                      lambda i: (i // tiles_per_mod, 0, i % tiles_per_mod, 0, 0)),
                  mspec, pl.BlockSpec((D_MODEL, D_MODEL), const), pl.BlockSpec((1, D_MODEL), const),
                  pl.BlockSpec((2, D_MODEL, LANES), lambda i: (0, 0, 0)), pl.BlockSpec((1, LANES), const)],
        out_specs=[xspec, pl.BlockSpec((tt * ROW_TILES, LANES), lambda i: (i, 0)),
                   pl.BlockSpec((tt, LANES), lambda i: (i, 0))],
        out_shape=[jax.ShapeDtypeStruct((n_tok, D_MODEL), F32),
                   jax.ShapeDtypeStruct((n_tok * ROW_TILES, LANES), F32),
                   jax.ShapeDtypeStruct((n_tok, LANES), F32)],
        compiler_params=_params(1),
        name="o_proj_tail",
    )(x, o_t, mods, wo, gf, wr, br)


def _bpost_kernel(x_ref, o_ref, mod_ref, wo_ref, gf_ref, wr_ref, br_ref, x1_ref, h2_ref, lg_ref):
    y = jnp.dot(o_ref[...], wo_ref[...], preferred_element_type=F32)
    x1 = x_ref[...] + mod_ref[0] * y
    _tail(x1, gf_ref, mod_ref[1], mod_ref[2], wr_ref, br_ref, x1_ref, h2_ref, lg_ref)


def _bpost_call(x, o, mods, wo, gf, wr, br, tiles_per_mod):
    n_tok = x.shape[0]
    tt = min(n_tok, TOKEN_TILE)
    xspec, mspec = _tok_specs(tt, mods, tiles_per_mod)
    const = lambda i: (0, 0)
    return pl.pallas_call(
        _bpost_kernel,
        grid=(n_tok // tt,),
        in_specs=[xspec, xspec, mspec, pl.BlockSpec((D_MODEL, D_MODEL), const), pl.BlockSpec((1, D_MODEL), const),
                  pl.BlockSpec((2, D_MODEL, LANES), lambda i: (0, 0, 0)), pl.BlockSpec((1, LANES), const)],
        out_specs=[xspec, pl.BlockSpec((tt * ROW_TILES, LANES), lambda i: (i, 0)),
                   pl.BlockSpec((tt, LANES), lambda i: (i, 0))],
        out_shape=[jax.ShapeDtypeStruct((n_tok, D_MODEL), F32),
                   jax.ShapeDtypeStruct((n_tok * ROW_TILES, LANES), F32),
                   jax.ShapeDtypeStruct((n_tok, LANES), F32)],
        compiler_params=_params(1),
        name="o_proj_tail",
    )(x, o, mods, wo, gf, wr, br)


def _final_kernel(x_ref, mod_ref, g_ref, y_out):
    y_out[...] = _norm_mod(x_ref[...], g_ref[...], mod_ref[0], mod_ref[1])


def _final_call(x, mods, g, tiles_per_mod):
    n_tok = x.shape[0]
    tt = min(n_tok, TOKEN_TILE)
    xspec, mspec = _tok_specs(tt, mods, tiles_per_mod)
    return pl.pallas_call(
        _final_kernel,
        grid=(n_tok // tt,),
        in_specs=[xspec, mspec, pl.BlockSpec((1, D_MODEL), lambda i: (0, 0))],
        out_specs=xspec,
        out_shape=jax.ShapeDtypeStruct((n_tok, D_MODEL), F32),
        compiler_params=_params(1),
        name="final_norm",
    )(x, mods, g)


def _route_kernel(lg_ref, idx_ref, gate_ref, rank_ref, cnt_ref, carry, *, tr):
    i = pl.program_id(0)

    @pl.when(i == 0)
    def _():
        carry[...] = jnp.zeros_like(carry)

    lane = lax.broadcasted_iota(jnp.int32, (tr, LANES), 1)
    lanef = lane.astype(F32)
    logit = jnp.where(lane < N_EXPERTS, lg_ref[...], -jnp.inf)
    r_i = lax.broadcasted_iota(jnp.int32, (tr, tr), 0)
    c_i = lax.broadcasted_iota(jnp.int32, (tr, tr), 1)
    earlier = jnp.where(c_i < r_i, 1.0, 0.0).astype(BF16)
    base = carry[...]
    idx_out = jnp.zeros((tr, LANES), F32)
    rank_out = jnp.zeros((tr, LANES), F32)
    val_out = jnp.zeros((tr, LANES), F32)
    top0 = None
    for k in range(TOP_K):
        m = jnp.max(logit, axis=1, keepdims=True)
        sel = jnp.min(jnp.where(logit == m, lanef, float(LANES)), axis=1, keepdims=True)
        onehot = lanef == sel
        ohf = jnp.where(onehot, 1.0, 0.0)
        before = jnp.dot(earlier, ohf.astype(BF16), preferred_element_type=F32)
        rank = jnp.sum(jnp.where(onehot, before + base, 0.0), axis=1, keepdims=True)
        base = base + jnp.sum(ohf, axis=0, keepdims=True)
        logit = jnp.where(onehot, -jnp.inf, logit)
        if k == 0:
            top0 = m
        idx_out = jnp.where(lane == k, sel, idx_out)
        rank_out = jnp.where(lane == k, rank, rank_out)
        val_out = jnp.where(lane == k, jnp.exp(m - top0), val_out)
    carry[...] = base
    cnt_ref[...] = base
    idx_ref[...] = idx_out
    rank_ref[...] = rank_out
    gate_ref[...] = val_out / jnp.sum(val_out, axis=1, keepdims=True)


def _route_call(logits):
    n_tok = logits.shape[0]
    tr = min(n_tok, TOKEN_TILE)
    spec = pl.BlockSpec((tr, LANES), lambda i: (i, 0))
    sds = jax.ShapeDtypeStruct((n_tok, LANES), F32)
    return pl.pallas_call(
        functools.partial(_route_kernel, tr=tr),
        grid=(n_tok // tr,),
        in_specs=[spec],
        out_specs=[spec, spec, spec, pl.BlockSpec((1, LANES), lambda i: (0, 0))],
        out_shape=[sds, sds, sds, jax.ShapeDtypeStruct((1, LANES), F32)],
        scratch_shapes=[pltpu.VMEM((1, LANES), F32)],
        compiler_params=_params(1),
        name="route",
    )(logits)


def _row_copy(src_hbm, src_row, dst_hbm, dst_row, sem):
    return pltpu.make_async_copy(src_hbm.at[pl.ds(pl.multiple_of(src_row, ROW_TILES), ROW_TILES), :],
                                 dst_hbm.at[pl.ds(pl.multiple_of(dst_row, ROW_TILES), ROW_TILES), :], sem)


def _dispatch_kernel(dst_ref, h_ref, xs_in_hbm, xs_hbm, sem, *, chunk):
    del xs_in_hbm

    def issue(g, carry):
        tok = g * (DMA_UNROLL // TOP_K)
        for u in range(DMA_UNROLL):
            _row_copy(h_ref, (tok + u // TOP_K) * ROW_TILES, xs_hbm, dst_ref[0, g * DMA_UNROLL + u], sem).start()
        return carry

    def drain(g, carry):
        for u in range(DMA_UNROLL):
            _row_copy(h_ref, 0, xs_hbm, 0, sem).wait()
        return carry

    lax.fori_loop(0, chunk // DMA_UNROLL, issue, 0)
    lax.fori_loop(0, chunk // DMA_UNROLL, drain, 0)


def _dispatch_call(h2_tiles, dst_rows, xs):
    n_asg = dst_rows.shape[0]
    chunk = min(n_asg, DISPATCH_CHUNK)
    any_spec = pl.BlockSpec(memory_space=pl.ANY)
    return pl.pallas_call(
        functools.partial(_dispatch_kernel, chunk=chunk),
        grid=(n_asg // chunk,),
        in_specs=[pl.BlockSpec((None, 1, chunk), lambda i: (i, 0, 0), memory_space=pltpu.SMEM),
                  pl.BlockSpec((chunk // TOP_K * ROW_TILES, LANES), lambda i: (i, 0)), any_spec],
        out_specs=any_spec,
        out_shape=jax.ShapeDtypeStruct(xs.shape, F32),
        scratch_shapes=[pltpu.SemaphoreType.DMA(())],
        input_output_aliases={2: 0},
        compiler_params=_params(1),
        name="moe_dispatch",
    )(dst_rows.reshape(n_asg // chunk, 1, chunk), h2_tiles, xs)


def _moe_kernel(be_ref, nv_ref, x_ref, wgu_ref, bgu_ref, wd_ref, bd_ref, o_ref, wgu_bf, wd_bf, *, tm):
    i = pl.program_id(0)
    active = i < nv_ref[0]
    fresh = jnp.logical_or(i == 0, be_ref[i] != be_ref[jnp.maximum(i - 1, 0)])

    @pl.when(jnp.logical_and(active, fresh))
    def _():
        for c in range(2 * D_FF // FF_CHUNK):
            wgu_bf[:, c * FF_CHUNK:(c + 1) * FF_CHUNK] = wgu_ref[:, c * FF_CHUNK:(c + 1) * FF_CHUNK].astype(BF16)
        for c in range(D_MODEL // FF_CHUNK):
            wd_bf[:, c * FF_CHUNK:(c + 1) * FF_CHUNK] = wd_ref[:, c * FF_CHUNK:(c + 1) * FF_CHUNK].astype(BF16)

    @pl.when(active)
    def _():
        x = _load_row_tiles(x_ref, tm).astype(BF16)
        acc = None
        for c in range(D_FF // FF_CHUNK):
            lo = c * FF_CHUNK
            glu = jnp.dot(x, wgu_bf[:, lo:lo + FF_CHUNK], preferred_element_type=F32) + bgu_ref[:, lo:lo + FF_CHUNK]
            lin = (jnp.dot(x, wgu_bf[:, D_FF + lo:D_FF + lo + FF_CHUNK], preferred_element_type=F32)
                   + bgu_ref[:, D_FF + lo:D_FF + lo + FF_CHUNK])
            glu = jnp.minimum(glu, SWIGLU_LIMIT)
            lin = jnp.clip(lin, -SWIGLU_LIMIT, SWIGLU_LIMIT)
            act = glu * jax.nn.sigmoid(SWIGLU_ALPHA * glu) * (lin + 1.0)
            part = jnp.dot(act.astype(BF16), wd_bf[lo:lo + FF_CHUNK, :], preferred_element_type=F32)
            acc = part if acc is None else acc + part
        _store_row_tiles(o_ref, acc + bd_ref[...])

    @pl.when(jnp.logical_not(active))
    def _():
        o_ref[...] = jnp.zeros_like(o_ref)


def _moe_call(xs, blk_expert, n_valid, layer, wgu, bgu, wd, bd, tm):
    n_blocks = blk_expert.shape[0]
    rspec = pl.BlockSpec((tm * ROW_TILES, LANES), lambda i, be, nv: (i, 0))
    emap = lambda i, be, nv: (layer, be[i], 0, 0)
    grid_spec = pltpu.PrefetchScalarGridSpec(
        num_scalar_prefetch=2,
        grid=(n_blocks,),
        in_specs=[rspec,
                  pl.BlockSpec((None, None, D_MODEL, 2 * D_FF), emap),
                  pl.BlockSpec((None, None, 1, 2 * D_FF), emap),
                  pl.BlockSpec((None, None, D_FF, D_MODEL), emap),
                  pl.BlockSpec((None, None, 1, D_MODEL), emap)],
        out_specs=rspec,
        scratch_shapes=[pltpu.VMEM((D_MODEL, 2 * D_FF), BF16), pltpu.VMEM((D_FF, D_MODEL), BF16)],
    )
    return pl.pallas_call(
        functools.partial(_moe_kernel, tm=tm),
        grid_spec=grid_spec,
        out_shape=jax.ShapeDtypeStruct((n_blocks * tm * ROW_TILES, LANES), F32),
        compiler_params=_params(1),
        name="moe_experts",
    )(blk_expert, n_valid, xs, wgu, bgu, wd, bd)


def _combine_kernel(cur_ref, nxt_ref, x_ref, gate_ref, g2_ref, out_hbm, x2_ref, buf, sems, *, tt, n_tiles):
    i = pl.program_id(0)
    n_copy = tt * TOP_K

    def gather(rows_ref, slot, wait):
        def body(g, carry):
            tok = g * (DMA_UNROLL // TOP_K)
            for u in range(DMA_UNROLL):
                if wait:
                    _row_copy(out_hbm, 0, buf.at[slot], 0, sems.at[slot]).wait()
                else:
                    dst = ((u % TOP_K) * tt + tok + u // TOP_K) * ROW_TILES
                    _row_copy(out_hbm, rows_ref[0, g * DMA_UNROLL + u], buf.at[slot], dst, sems.at[slot]).start()
            return carry
        lax.fori_loop(0, n_copy // DMA_UNROLL, body, 0)

    slot = i % 2

    @pl.when(i == 0)
    def _():
        gather(cur_ref, 0, False)

    @pl.when(i + 1 < n_tiles)
    def _():
        gather(nxt_ref, 1 - slot, False)

    gather(cur_ref, slot, True)
    gates = gate_ref[...]
    gate_k = [jnp.broadcast_to(gates[:, k:k + 1], (tt, LANES)) for k in range(TOP_K)]
    rows = buf.at[slot]
    for j in range(ROW_TILES):
        cols = slice(j * LANES, (j + 1) * LANES)
        y = None
        for k in range(TOP_K):
            term = gate_k[k] * rows[pl.ds(k * tt * ROW_TILES + j, tt, stride=ROW_TILES), :]
            y = term if y is None else y + term
        x2_ref[:, cols] = x_ref[:, cols] + g2_ref[0, :, cols] * y


def _combine_call(x1, gate, g2, out_tiles, pos_rows, tiles_per_mod_512):
    n_tok = x1.shape[0]
    tt = min(n_tok, COMBINE_TILE)
    n_tiles = n_tok // tt
    rows_per_mod = g2.shape[2]
    tiles_per_mod = tiles_per_mod_512 * (TOKEN_TILE // tt) if rows_per_mod == 1 else 1
    if rows_per_mod == 1:
        g2spec = pl.BlockSpec((None, 1, 1, D_MODEL), lambda i: (i // tiles_per_mod, 0, 0, 0))
    else:
        g2spec = pl.BlockSpec((None, 1, tt, D_MODEL), lambda i: (0, 0, i, 0))
    pos3 = pos_rows.reshape(n_tiles, 1, tt * TOP_K)
    pspec = lambda f: pl.BlockSpec((None, 1, tt * TOP_K), f, memory_space=pltpu.SMEM)
    return pl.pallas_call(
        functools.partial(_combine_kernel, tt=tt, n_tiles=n_tiles),
        grid=(n_tiles,),
        in_specs=[pspec(lambda i: (i, 0, 0)), pspec(lambda i: (jnp.minimum(i + 1, n_tiles - 1), 0, 0)),
                  pl.BlockSpec((tt, D_MODEL), lambda i: (i, 0)), pl.BlockSpec((tt, LANES), lambda i: (i, 0)),
                  g2spec, pl.BlockSpec(memory_space=pl.ANY)],
        out_specs=pl.BlockSpec((tt, D_MODEL), lambda i: (i, 0)),
        out_shape=jax.ShapeDtypeStruct((n_tok, D_MODEL), F32),
        scratch_shapes=[pltpu.VMEM((2, tt * TOP_K * ROW_TILES, LANES), F32), pltpu.SemaphoreType.DMA((2,))],
        compiler_params=_params(1),
        name="moe_combine",
    )(pos3, pos3, x1, gate, g2, out_tiles)


def _moe(x1, h2_tiles, logits, g2, xs, layer, experts, tm, tiles_per_mod):
    n_tok = x1.shape[0]
    idx_f, gate_f, rank_f, cnt_f = _route_call(logits)
    idx = idx_f[:, :TOP_K].astype(jnp.int32)
    rank = rank_f[:, :TOP_K].astype(jnp.int32)
    counts = cnt_f[0, :N_EXPERTS].astype(jnp.int32)
    padded = (counts + tm - 1) // tm * tm
    pad_end = jnp.cumsum(padded)
    pad_start = pad_end - padded
    pos_rows = ((pad_start[idx] + rank) * ROW_TILES).reshape(-1)
    n_blocks = n_tok * TOP_K // tm + N_EXPERTS
    blk_start = jnp.arange(n_blocks, dtype=jnp.int32) * tm
    blk_expert = jnp.minimum(jnp.sum(pad_end[None, :] <= blk_start[:, None], axis=1), N_EXPERTS - 1).astype(jnp.int32)
    n_valid = (pad_end[-1] // tm).astype(jnp.int32).reshape(1)
    xs = _dispatch_call(h2_tiles, pos_rows, xs)
    out = _moe_call(xs, blk_expert, n_valid, layer, *experts, tm)
    return _combine_call(x1, gate_f, g2, out, pos_rows, tiles_per_mod), xs


def _cumsum_kernel(x_ref, o_ref, carry, *, tc):
    j = pl.program_id(1)

    @pl.when(j == 0)
    def _():
        carry[...] = jnp.zeros_like(carry)

    r_i = lax.broadcasted_iota(jnp.int32, (tc, tc), 0)
    c_i = lax.broadcasted_iota(jnp.int32, (tc, tc), 1)
    upto = jnp.where(r_i <= c_i, 1.0, 0.0)
    cs = jnp.dot(x_ref[...], upto, preferred_element_type=F32, precision=HIGHEST) + carry[...]
    o_ref[...] = cs
    carry[...] = cs[:, tc - 1:tc]


def _cumsum_call(x):
    rows, tk = x.shape
    tc = LANES
    tr = min(rows, LANES)
    spec = pl.BlockSpec((tr, tc), lambda r, j: (r, j))
    return pl.pallas_call(
        functools.partial(_cumsum_kernel, tc=tc),
        grid=(rows // tr, tk // tc),
        in_specs=[spec],
        out_specs=spec,
        out_shape=jax.ShapeDtypeStruct(x.shape, F32),
        scratch_shapes=[pltpu.VMEM((tr, 1), F32)],
        compiler_params=_params(2),
        name="logf_cumsum",
    )(x)


ATTN_K_ROWS = LANES
ATTN_V_ROWS = HEAD_DIM + 16


def _attn_kernel(q_ref, k_ref, v_ref, o_ref, acc_sc, *, tq, tk, pos0, nk):
    qi = pl.program_id(2)
    q_first = pos0 + qi * tq
    n_full = jnp.minimum(nk, (q_first + 1) // tk)
    n_end = jnp.minimum(nk, (q_first + tq - 1) // tk + 1)
    acc_sc[...] = jnp.zeros_like(acc_sc)
    width = KV_GROUP * tq

    def block(ki, m, masked):
        s = jnp.dot(k_ref[ki], q_ref[...], preferred_element_type=F32)
        if masked:
            key = ki * tk + lax.broadcasted_iota(jnp.int32, (tk, width), 0)
            qry = q_first + (lax.broadcasted_iota(jnp.int32, (tk, width), 1) & (tq - 1))
            s = jnp.where(key <= qry, s, NEG_INF)
        m_new = jnp.maximum(m, jnp.max(s, axis=0, keepdims=True))
        alpha = jnp.exp(m - m_new)
        p = jnp.exp(s - m_new).astype(BF16)
        acc_sc[...] = alpha * acc_sc[...] + jnp.dot(v_ref[ki], p, preferred_element_type=F32)
        return m_new

    m = jnp.full((1, width), NEG_INF, F32)
    m = lax.fori_loop(0, n_full, lambda ki, c: block(ki, c, False), m)
    lax.fori_loop(n_full, n_end, lambda ki, c: block(ki, c, True), m)
    a = acc_sc[...]
    o_ref[...] = (a[:HEAD_DIM] / a[HEAD_DIM:HEAD_DIM + 1]).astype(BF16)


def _attn_call(q_aug, k_aug, v_aug, pos0):
    bsz, _, nq, _, width = q_aug.shape
    nk, tk = k_aug.shape[2], k_aug.shape[3]
    tq = width // KV_GROUP
    assert tq & (tq - 1) == 0
    return pl.pallas_call(
        functools.partial(_attn_kernel, tq=tq, tk=tk, pos0=pos0, nk=nk),
        grid=(bsz, N_KV_HEADS, nq),
        in_specs=[pl.BlockSpec((None, None, None, ATTN_K_ROWS, width), lambda b, h, qi: (b, h, qi, 0, 0)),
                  pl.BlockSpec((None, None, nk, tk, ATTN_K_ROWS), lambda b, h, qi: (b, h, 0, 0, 0)),
                  pl.BlockSpec((None, None, nk, ATTN_V_ROWS, tk), lambda b, h, qi: (b, h, 0, 0, 0))],
        out_specs=pl.BlockSpec((None, None, None, HEAD_DIM, width), lambda b, h, qi: (b, h, qi, 0, 0)),
        out_shape=jax.ShapeDtypeStruct((bsz, N_KV_HEADS, nq, HEAD_DIM, width), BF16),
        scratch_shapes=[pltpu.VMEM((ATTN_V_ROWS, width), F32)],
        compiler_params=_params(3),
        name="forget_attention",
    )(q_aug, k_aug, v_aug)


def _split3(c):
    def top(x):
        bits = lax.bitcast_convert_type(x, jnp.uint32) & jnp.uint32(0xFFFF0000)
        return lax.bitcast_convert_type(bits, F32)
    hi = top(c)
    mid = top(c - hi)
    lo = top(c - hi - mid)
    return hi.astype(BF16), mid.astype(BF16), lo.astype(BF16)


def _attn_kv_operands(k_all, v_all, cum, tk):
    bsz, seq_k = k_all.shape[:2]
    nk = seq_k // tk
    kt = k_all.astype(BF16).transpose(0, 2, 1, 3)
    ck = jnp.stack(_split3(-cum.reshape(bsz, N_KV_HEADS, KV_GROUP, seq_k)), axis=3)
    ck = ck.reshape(bsz, N_KV_HEADS, 3 * KV_GROUP, seq_k).transpose(0, 1, 3, 2)
    ones = jnp.ones((bsz, N_KV_HEADS, seq_k, 3), BF16)
    zpad = jnp.zeros((bsz, N_KV_HEADS, seq_k, ATTN_K_ROWS - HEAD_DIM - 3 - 3 * KV_GROUP), BF16)
    k_aug = jnp.concatenate([kt, ones, ck, zpad], axis=3).reshape(bsz, N_KV_HEADS, nk, tk, ATTN_K_ROWS)
    vt = v_all.astype(BF16).transpose(0, 2, 3, 1)
    v_aug = jnp.concatenate([vt, jnp.ones((bsz, N_KV_HEADS, 1, seq_k), BF16),
                             jnp.zeros((bsz, N_KV_HEADS, ATTN_V_ROWS - HEAD_DIM - 1, seq_k), BF16)], axis=2)
    v_aug = v_aug.reshape(bsz, N_KV_HEADS, ATTN_V_ROWS, nk, tk).transpose(0, 1, 3, 2, 4)
    return k_aug, v_aug


def _attn_q_operand(q, cum_q, tq):
    bsz, seq = q.shape[:2]
    nq, width = seq // tq, KV_GROUP * tq
    lead = (bsz, N_KV_HEADS, nq)
    qt = q.reshape(bsz, nq, tq, N_KV_HEADS, KV_GROUP, HEAD_DIM).transpose(0, 3, 1, 5, 4, 2).reshape(*lead, HEAD_DIM, width)
    to_lanes = lambda c: c.reshape(bsz, N_KV_HEADS, KV_GROUP, nq, tq).transpose(0, 1, 3, 2, 4).reshape(*lead, 1, width)
    cq = [to_lanes(c) for c in _split3(cum_q)]
    sel = jnp.repeat(jnp.repeat(jnp.eye(KV_GROUP, dtype=BF16), 3, axis=0), tq, axis=1)
    sel = jnp.broadcast_to(sel, (*lead, 3 * KV_GROUP, width))
    zpad = jnp.zeros((*lead, ATTN_K_ROWS - HEAD_DIM - 3 - 3 * KV_GROUP, width), BF16)
    return jnp.concatenate([qt, *cq, sel, zpad], axis=3)


def _attn_untranspose(o_t, tq):
    bsz, _, nq = o_t.shape[:3]
    o = o_t.reshape(bsz, N_KV_HEADS, nq, HEAD_DIM, KV_GROUP, tq).transpose(0, 2, 5, 1, 4, 3)
    return o.reshape(bsz * nq * tq, D_MODEL)


def _split_mods(m, n, expand_t):
    bsz = m.shape[0]
    m = m.reshape(bsz, n, 1, D_MODEL)
    if expand_t:
        m = jnp.broadcast_to(m.transpose(1, 0, 2, 3), (n, bsz, expand_t, D_MODEL)).reshape(1, n, bsz * expand_t, D_MODEL)
    return m


def _trunk(x, ada, ada_kv, ada_final, pool_hist, kv_past, pos0, w):
    bsz, seq, _ = x.shape
    n_tok = bsz * seq
    flat_small = seq < TOKEN_TILE
    expand_t = seq if flat_small else 0
    tiles_per_mod = 1 if flat_small else seq // TOKEN_TILE
    tm = MOE_TILE if n_tok * TOP_K >= 64 * MOE_TILE else 128
    hist16 = jnp.pad(pool_hist, ((0, 0), (0, 0), (1, 0), (0, 0)))
    xs = jnp.zeros(((n_tok * TOP_K + N_EXPERTS * tm) * ROW_TILES, LANES), F32)
    new_pool = []
    xf = x.reshape(n_tok, D_MODEL)
    k4 = v4 = logf_new = k_aug = v_aug = cum_q = None
    for layer in range(DEPTH):
        lw = w["layers"][layer]
        m6 = ada[layer]
        if layer < N_A_LAYERS:
            mods = _split_mods(m6[:, :5 * D_MODEL], 5, 0)
            x1, h2, logits, hout = _pool_call(xf.reshape(bsz, seq, D_MODEL), hist16[layer], mods, lw["gm"], lw["gf"],
                                              lw["w_pool"], lw["pool_scale"], lw["wr"], lw["br"], pos0)
            new_pool.append(hout[:, 1:])
            x1 = x1.reshape(n_tok, D_MODEL)
        else:
            mods = _split_mods(m6[:, :5 * D_MODEL], 5, expand_t)
            tq = min(seq, ATTN_TQ)
            if flat_small:
                q = _bpre_call(xf, mods[:, 0:2], lw["gm"], lw["wq"], tiles_per_mod)
                o_t = _attn_call(_attn_q_operand(q.reshape(bsz, seq, D_MODEL), cum_q, tq), k_aug, v_aug, pos0)
                x1, h2, logits = _bpost_call(xf, _attn_untranspose(o_t, tq), mods[:, 2:5], lw["wo"], lw["gf"],
                                             lw["wr"], lw["br"], tiles_per_mod)
            else:
                q_aug = _q_aug_call(xf, mods[:, 0:2], lw["gm"], lw["wqt"], cum_q, tq, tiles_per_mod)
                o_t = _attn_call(q_aug, k_aug, v_aug, pos0)
                x1, h2, logits = _bpost_t_call(xf, o_t, mods[:, 2:5], lw["wo"], lw["gf"], lw["wr"], lw["br"],
                                               tiles_per_mod)
        g2 = _split_mods(m6[:, 5 * D_MODEL:], 1, expand_t)
        xf, xs = _moe(x1, h2, logits, g2, xs, layer, w["experts"], tm, tiles_per_mod)
        if layer == N_A_LAYERS - 1:
            mods_kv = _split_mods(ada_kv, 2, expand_t)
            k_new, v_new, lf = _kv_call(xf, mods_kv, w["g_kv"], w["wk"], w["wv"], w["wf"], w["bf"], tiles_per_mod)
            logf_new = lf[:, :N_HEADS].reshape(bsz, seq, N_HEADS)
            k4 = k_new.reshape(bsz, seq, N_KV_HEADS, HEAD_DIM)
            v4 = v_new.reshape(bsz, seq, N_KV_HEADS, HEAD_DIM)
            if kv_past is None:
                k_all, v_all, logf_all = k4, v4, logf_new
            else:
                k_all = jnp.concatenate([kv_past[0], k4], axis=1)
                v_all = jnp.concatenate([kv_past[1], v4], axis=1)
                logf_all = jnp.concatenate([kv_past[2], logf_new], axis=1)
            seq_k = k_all.shape[1]
            seq_kp = -(-seq_k // LANES) * LANES
            padk = ((0, 0), (0, seq_kp - seq_k), (0, 0), (0, 0))
            lf_t = jnp.pad(logf_all, padk[:3]).transpose(0, 2, 1).reshape(bsz * N_HEADS, seq_kp)
            cum = _cumsum_call(lf_t).reshape(bsz, N_HEADS, seq_kp)
            cum_q = cum[:, :, seq_k - seq:seq_k]
            tk = ATTN_TK if (seq >= ATTN_TQ and seq_kp % ATTN_TK == 0) else seq_kp
            k_aug, v_aug = _attn_kv_operands(jnp.pad(k_all, padk), jnp.pad(v_all, padk), cum, tk)
    mods_f = _split_mods(ada_final, 2, expand_t)
    y_out = _final_call(xf, mods_f, w["g_final"], tiles_per_mod)
    return (y_out.reshape(bsz, seq, D_MODEL), k4, v4, logf_new, jnp.stack(new_pool, axis=0))


def _pad_lanes(a, axis):
    pad = [(0, 0)] * a.ndim
    pad[axis] = (0, LANES - a.shape[axis])
    return jnp.pad(a, pad)


def _router_operand(w):
    w = _pad_lanes(w, 1)
    hi = _top_bits(w)
    return jnp.stack([hi.astype(BF16), (w - hi).astype(BF16)])


def kernel(x_prompt, x_sample, cache_k, cache_v, cache_logf, state_pool, c_prompt, c_sample, norm_mix_g, norm_ffn_g, w_ada, b_ada, w_pool, pool_scale, norm_kv_g, w_ada_kv, b_ada_kv, w_kv, b_forget, w_q, w_o, w_router, b_router, w_gate_up, b_gate_up, w_down, b_down, norm_final_g, w_ada_final, b_ada_final):
    bp = c_prompt.shape[0]
    c_all = jnp.concatenate([c_prompt, c_sample], axis=0)
    ada = _ada_call(c_all, w_ada, b_ada)
    w_small = jnp.stack([w_ada_kv, w_ada_final])
    b_small = jnp.stack([b_ada_kv, b_ada_final])
    ada_small = _ada_call(c_all, w_small, b_small)

    layers = []
    for layer in range(DEPTH):
        lw = dict(gm=norm_mix_g[layer][None], gf=norm_ffn_g[layer][None],
                  wr=_router_operand(w_router[layer]), br=_pad_lanes(b_router[layer][None], 1),
                  )
        if layer < N_A_LAYERS:
            lw.update(w_pool=w_pool[layer].astype(BF16), pool_scale=pool_scale[layer][None])
        else:
            j = layer - N_A_LAYERS
            lw.update(wq=w_q[j].astype(BF16), wqt=w_q[j].T.astype(BF16), wo=w_o[j].astype(BF16))
        layers.append(lw)
    w = dict(layers=layers, g_kv=norm_kv_g[None], g_final=norm_final_g[None],
             wk=w_kv[:, :KV_WIDTH].astype(BF16), wv=w_kv[:, KV_WIDTH:2 * KV_WIDTH].astype(BF16),
             wf=_pad_lanes(w_kv[:, 2 * KV_WIDTH:], 1), bf=_pad_lanes(b_forget[None], 1),
             experts=(w_gate_up, b_gate_up[:, :, None, :], w_down, b_down[:, :, None, :]))

    zero_hist = jnp.zeros((N_A_LAYERS, bp, POOL_HIST, D_MODEL), F32)
    y_p, k_p, v_p, lf_p, pool_p = _trunk(x_prompt, ada[:, :bp], ada_small[0, :bp], ada_small[1, :bp],
                                         zero_hist, None, 0, w)
    y_s, k_s, v_s, lf_s, pool_s = _trunk(x_sample, ada[:, bp:], ada_small[0, bp:], ada_small[1, bp:],
                                         state_pool, (cache_k, cache_v, cache_logf), cache_k.shape[1], w)
    return (y_p, y_s, k_p, v_p, lf_p, pool_p, k_s, v_s, lf_s, pool_s)
```

```python
import functools

import jax
import jax.numpy as jnp
from jax import lax
from jax.experimental import pallas as pl
from jax.experimental.pallas import tpu as pltpu

F32 = jnp.float32
BF16 = jnp.bfloat16
HIGHEST = lax.Precision.HIGHEST

D_MODEL = 1024
DEPTH = 4
N_A_LAYERS = DEPTH // 2
POOL_WINDOWS = (2, 4, 8, 16)
POOL_GROUP_DIM = D_MODEL // len(POOL_WINDOWS)
POOL_HIST = max(POOL_WINDOWS) - 1
HIST_ROWS = POOL_HIST + 1
N_HEADS = 16
HEAD_DIM = D_MODEL // N_HEADS
N_KV_HEADS = 4
KV_GROUP = N_HEADS // N_KV_HEADS
KV_WIDTH = N_KV_HEADS * HEAD_DIM
N_EXPERTS = 32
TOP_K = 4
D_FF = D_MODEL
SWIGLU_LIMIT = 7.0
SWIGLU_ALPHA = 1.702
RMS_EPS = 1e-6
NEG_INF = -1e30

LANES = 128
SUBLANES = 8
ROW_TILES = D_MODEL // LANES
TOKEN_TILE = 512
MOE_TILE = 512
FF_CHUNK = 512
DISPATCH_CHUNK = 2048
COMBINE_TILE = 256
DMA_UNROLL = 32
ATTN_TQ = 256
ATTN_TK = 256
VMEM_LIMIT = 56 * 1024 * 1024

assert ROW_TILES == SUBLANES


def _params(n_grid_dims):
    return pltpu.CompilerParams(dimension_semantics=("arbitrary",) * n_grid_dims, vmem_limit_bytes=VMEM_LIMIT)


def _norm_mod(x, gain, shift, scale):
    y = x * lax.rsqrt(jnp.mean(x * x, axis=-1, keepdims=True) + RMS_EPS) * gain
    return y * (1.0 + scale) + shift


def _store_row_tiles(ref, val):
    rows = val.shape[0]
    for j in range(ROW_TILES):
        ref[pl.ds(j, rows, stride=ROW_TILES), :] = val[:, j * LANES:(j + 1) * LANES]


def _load_row_tiles(ref, rows, first=0, stride=ROW_TILES):
    return jnp.concatenate([ref[pl.ds(first + j, rows, stride=stride), :] for j in range(ROW_TILES)], axis=1)


def _ada_kernel(c_ref, w_ref, b_ref, o_ref):
    c = c_ref[...]
    s = c * jax.nn.sigmoid(c)
    o_ref[...] = jnp.dot(s, w_ref[...], preferred_element_type=F32, precision=HIGHEST) + b_ref[...]


def _ada_call(c, w, b):
    n_layers, _, m = w.shape
    bc = c.shape[0]
    tn = 1024
    return pl.pallas_call(
        _ada_kernel,
        grid=(n_layers, m // tn),
        in_specs=[pl.BlockSpec((bc, D_MODEL), lambda l, j: (0, 0)),
                  pl.BlockSpec((None, D_MODEL, tn), lambda l, j: (l, 0, j)),
                  pl.BlockSpec((None, 1, tn), lambda l, j: (l, 0, j))],
        out_specs=pl.BlockSpec((None, bc, tn), lambda l, j: (l, 0, j)),
        out_shape=jax.ShapeDtypeStruct((n_layers, bc, m), F32),
        compiler_params=_params(2),
        name="ada",
    )(c, w, b.reshape(n_layers, 1, m))


def _top_bits(x):
    bits = lax.bitcast_convert_type(x, jnp.uint32) & jnp.uint32(0xFFFF0000)
    return lax.bitcast_convert_type(bits, F32)


def _tail(x1, gf_ref, sh2, sc2, wr_ref, br_ref, x1_ref, h2_ref, lg_ref):
    h2 = _norm_mod(x1, gf_ref[...], sh2, sc2)
    x1_ref[...] = x1
    _store_row_tiles(h2_ref, h2)
    hi = _top_bits(h2)
    lo = (h2 - hi).astype(BF16)
    hi = hi.astype(BF16)
    lg_ref[...] = (jnp.dot(hi, wr_ref[0], preferred_element_type=F32) + jnp.dot(hi, wr_ref[1], preferred_element_type=F32)
                   + jnp.dot(lo, wr_ref[0], preferred_element_type=F32) + br_ref[...])


def _pool_kernel(x_ref, hist_ref, mod_ref, gm_ref, gf_ref, wp_ref, ps_ref, wr_ref, br_ref,
                 x1_ref, h2_ref, lg_ref, hout_ref, ext, *, tt, pos0):
    t = pl.program_id(1)
    x = x_ref[...]
    h = _norm_mod(x, gm_ref[...], mod_ref[0], mod_ref[1])

    @pl.when(t == 0)
    def _():
        ext[0:HIST_ROWS, :] = hist_ref[...]

    ext[HIST_ROWS:HIST_ROWS + tt, :] = h
    pos = pos0 + t * tt + lax.broadcasted_iota(jnp.int32, (tt, 1), 0)
    ys = []
    for g, w in enumerate(POOL_WINDOWS):
        lo, hi = g * POOL_GROUP_DIM, (g + 1) * POOL_GROUP_DIM
        hg = h[:, lo:hi]
        win = hg
        for j in range(1, w):
            win = win + ext[HIST_ROWS - j:HIST_ROWS - j + tt, lo:hi]
        inv_cnt = 1.0 / jnp.minimum(pos + 1, w).astype(F32)
        u = win * inv_cnt - hg
        ys.append(jnp.dot(u.astype(BF16), wp_ref[g], preferred_element_type=F32))
    y = jnp.concatenate(ys, axis=-1) * ps_ref[...]
    x1 = x + mod_ref[2] * y
    last = ext[tt:tt + HIST_ROWS, :]
    ext[0:HIST_ROWS, :] = last
    hout_ref[...] = last
    _tail(x1, gf_ref, mod_ref[3], mod_ref[4], wr_ref, br_ref, x1_ref, h2_ref, lg_ref)


def _pool_call(x, hist16, mods, gm, gf, wp, ps, wr, br, pos0):
    bsz, seq, _ = x.shape
    tt = min(seq, TOKEN_TILE)
    nt = seq // tt
    row = lambda b, t: (b, t, 0)
    flat = lambda b, t: (b * nt + t, 0)
    const2 = lambda b, t: (0, 0)
    xspec = pl.BlockSpec((None, tt, D_MODEL), row)
    specs = [xspec,
             pl.BlockSpec((None, HIST_ROWS, D_MODEL), lambda b, t: (b, 0, 0)),
             pl.BlockSpec((None, mods.shape[1], 1, D_MODEL), lambda b, t: (b, 0, 0, 0)),
             pl.BlockSpec((1, D_MODEL), const2), pl.BlockSpec((1, D_MODEL), const2),
             pl.BlockSpec(wp.shape, lambda b, t: (0, 0, 0)),
             pl.BlockSpec((1, D_MODEL), const2),
             pl.BlockSpec((2, D_MODEL, LANES), lambda b, t: (0, 0, 0)), pl.BlockSpec((1, LANES), const2)]
    return pl.pallas_call(
        functools.partial(_pool_kernel, tt=tt, pos0=pos0),
        grid=(bsz, nt),
        in_specs=specs,
        out_specs=[xspec, pl.BlockSpec((tt * ROW_TILES, LANES), flat), pl.BlockSpec((tt, LANES), flat),
                   pl.BlockSpec((None, HIST_ROWS, D_MODEL), lambda b, t: (b, 0, 0))],
        out_shape=[jax.ShapeDtypeStruct((bsz, seq, D_MODEL), F32),
                   jax.ShapeDtypeStruct((bsz * seq * ROW_TILES, LANES), F32),
                   jax.ShapeDtypeStruct((bsz * seq, LANES), F32),
                   jax.ShapeDtypeStruct((bsz, HIST_ROWS, D_MODEL), F32)],
        scratch_shapes=[pltpu.VMEM((HIST_ROWS + tt, D_MODEL), F32)],
        compiler_params=_params(2),
        name="pool_layer",
    )(x, hist16, mods, gm, gf, wp, ps, wr, br)


def _tok_specs(tt, mods, tiles_per_mod):
    nmod, rows = mods.shape[1], mods.shape[2]
    xspec = pl.BlockSpec((tt, D_MODEL), lambda i: (i, 0))
    mspec = pl.BlockSpec((None, nmod, rows, D_MODEL), lambda i: (i // tiles_per_mod, 0, 0, 0))
    return xspec, mspec


def _kv_kernel(x_ref, mod_ref, g_ref, wk_ref, wv_ref, wf_ref, bf_ref, k_out, v_out, lf_out):
    h = _norm_mod(x_ref[...], g_ref[...], mod_ref[0], mod_ref[1])
    hb = h.astype(BF16)
    k_out[...] = jnp.dot(hb, wk_ref[...], preferred_element_type=F32)
    v_out[...] = jnp.dot(hb, wv_ref[...], preferred_element_type=F32)
    z = jnp.dot(h, wf_ref[...], preferred_element_type=F32, precision=HIGHEST) + bf_ref[...]
    lf_out[...] = jax.nn.log_sigmoid(z)


def _kv_call(x, mods, g, wk, wv, wf, bf, tiles_per_mod):
    n_tok = x.shape[0]
    tt = min(n_tok, TOKEN_TILE)
    xspec, mspec = _tok_specs(tt, mods, tiles_per_mod)
    const = lambda i: (0, 0)
    kvspec = pl.BlockSpec((tt, KV_WIDTH), lambda i: (i, 0))
    return pl.pallas_call(
        _kv_kernel,
        grid=(n_tok // tt,),
        in_specs=[xspec, mspec, pl.BlockSpec((1, D_MODEL), const),
                  pl.BlockSpec((D_MODEL, KV_WIDTH), const), pl.BlockSpec((D_MODEL, KV_WIDTH), const),
                  pl.BlockSpec((D_MODEL, LANES), const), pl.BlockSpec((1, LANES), const)],
        out_specs=[kvspec, kvspec, pl.BlockSpec((tt, LANES), lambda i: (i, 0))],
        out_shape=[jax.ShapeDtypeStruct((n_tok, KV_WIDTH), F32),
                   jax.ShapeDtypeStruct((n_tok, KV_WIDTH), F32),
                   jax.ShapeDtypeStruct((n_tok, LANES), F32)],
        compiler_params=_params(1),
        name="kv_proj",
    )(x, mods, g, wk, wv, wf, bf)


def _bpre_kernel(x_ref, mod_ref, g_ref, wq_ref, q_out):
    h = _norm_mod(x_ref[...], g_ref[...], mod_ref[0], mod_ref[1])
    q = jnp.dot(h.astype(BF16), wq_ref[...], preferred_element_type=F32)
    q_out[...] = (q * (HEAD_DIM ** -0.5)).astype(BF16)


def _bpre_call(x, mods, g, wq, tiles_per_mod):
    n_tok = x.shape[0]
    tt = min(n_tok, TOKEN_TILE)
    xspec, mspec = _tok_specs(tt, mods, tiles_per_mod)
    const = lambda i: (0, 0)
    return pl.pallas_call(
        _bpre_kernel,
        grid=(n_tok // tt,),
        in_specs=[xspec, mspec, pl.BlockSpec((1, D_MODEL), const), pl.BlockSpec((D_MODEL, D_MODEL), const)],
        out_specs=xspec,
        out_shape=jax.ShapeDtypeStruct((n_tok, D_MODEL), BF16),
        compiler_params=_params(1),
        name="q_proj",
    )(x, mods, g, wq)


def _q_aug_kernel(x_ref, mod_ref, g_ref, wqt_ref, cum_ref, q_out, *, tq):
    tt = x_ref.shape[0]
    width = KV_GROUP * tq
    h = _norm_mod(x_ref[...], g_ref[...], mod_ref[0], mod_ref[1]).astype(BF16)
    qt = lax.dot_general(wqt_ref[...], h, (((1,), (1,)), ((), ())), preferred_element_type=F32) * (HEAD_DIM ** -0.5)
    row = lax.broadcasted_iota(jnp.int32, (16, width), 0)
    lane = lax.broadcasted_iota(jnp.int32, (16, width), 1)
    sel = jnp.zeros((16, width), F32)
    for g in range(KV_GROUP):
        own = (row >= 3 + 3 * g) & (row < 6 + 3 * g) & (lane >= g * tq) & (lane < (g + 1) * tq)
        sel = jnp.where(own, 1.0, sel)
    for kv in range(N_KV_HEADS):
        for qq in range(tt // tq):
            cols = slice(qq * tq, (qq + 1) * tq)
            for g in range(KV_GROUP):
                r0 = (kv * KV_GROUP + g) * HEAD_DIM
                q_out[kv, qq, 0:HEAD_DIM, g * tq:(g + 1) * tq] = qt[r0:r0 + HEAD_DIM, cols].astype(BF16)
            c = jnp.concatenate([cum_ref[kv * KV_GROUP + g:kv * KV_GROUP + g + 1, cols] for g in range(KV_GROUP)], axis=1)
            hi = _top_bits(c)
            mid = _top_bits(c - hi)
            lo = _top_bits(c - hi - mid)
            bias = jnp.where(row == 0, hi, jnp.where(row == 1, mid, jnp.where(row == 2, lo, sel)))
            q_out[kv, qq, HEAD_DIM:HEAD_DIM + 16, :] = bias.astype(BF16)
            q_out[kv, qq, HEAD_DIM + 16:, :] = jnp.zeros((ATTN_K_ROWS - HEAD_DIM - 16, width), BF16)


def _q_aug_call(x, mods, g, wqt, cum_q, tq, tiles_per_mod):
    n_tok = x.shape[0]
    bsz, _, seq = cum_q.shape
    tt = TOKEN_TILE
    xspec, mspec = _tok_specs(tt, mods, tiles_per_mod)
    const = lambda i: (0, 0)
    width = KV_GROUP * tq
    return pl.pallas_call(
        functools.partial(_q_aug_kernel, tq=tq),
        grid=(n_tok // tt,),
        in_specs=[xspec, mspec, pl.BlockSpec((1, D_MODEL), const), pl.BlockSpec((D_MODEL, D_MODEL), const),
                  pl.BlockSpec((None, N_HEADS, tt), lambda i: (i // tiles_per_mod, 0, i % tiles_per_mod))],
        out_specs=pl.BlockSpec((None, N_KV_HEADS, tt // tq, ATTN_K_ROWS, width),
                               lambda i: (i // tiles_per_mod, 0, i % tiles_per_mod, 0, 0)),
        out_shape=jax.ShapeDtypeStruct((bsz, N_KV_HEADS, seq // tq, ATTN_K_ROWS, width), BF16),
        compiler_params=_params(1),
        name="q_proj_aug",
    )(x, mods, g, wqt, cum_q)


def _bpost_t_kernel(x_ref, ot_ref, mod_ref, wo_ref, gf_ref, wr_ref, br_ref, x1_ref, h2_ref, lg_ref, *, tq):
    n_q = ot_ref.shape[1]
    rows = []
    for kv in range(N_KV_HEADS):
        for g in range(KV_GROUP):
            rows.append(jnp.concatenate([ot_ref[kv, qq, :, g * tq:(g + 1) * tq] for qq in range(n_q)], axis=1))
    o_t = jnp.concatenate(rows, axis=0)
    y = lax.dot_general(o_t, wo_ref[...], (((0,), (0,)), ((), ())), preferred_element_type=F32)
    x1 = x_ref[...] + mod_ref[0] * y
    _tail(x1, gf_ref, mod_ref[1], mod_ref[2], wr_ref, br_ref, x1_ref, h2_ref, lg_ref)


def _bpost_t_call(x, o_t, mods, wo, gf, wr, br, tiles_per_mod):
    n_tok = x.shape[0]
    tt = TOKEN_TILE
    width = o_t.shape[-1]
    tq = width // KV_GROUP
    xspec, mspec = _tok_specs(tt, mods, tiles_per_mod)
    const = lambda i: (0, 0)
    return pl.pallas_call(
        functools.partial(_bpost_t_kernel, tq=tq),
        grid=(n_tok // tt,),
        in_specs=[xspec, pl.BlockSpec((None, N_KV_HEADS, tt // tq, HEAD_DIM, width),
                                      lambda i: (i // tiles_per_mod, 0, i % tiles_per_mod, 0, 0)),
                  mspec, pl.BlockSpec((D_MODEL, D_MODEL), const), pl.BlockSpec((1, D_MODEL), const),
                  pl.BlockSpec((2, D_MODEL, LANES), lambda i: (0, 0, 0)), pl.BlockSpec((1, LANES), const)],
        out_specs=[xspec, pl.BlockSpec((tt * ROW_TILES, LANES), lambda i: (i, 0)),
                   pl.BlockSpec((tt, LANES), lambda i: (i, 0))],
        out_shape=[jax.ShapeDtypeStruct((n_tok, D_MODEL), F32),
                   jax.ShapeDtypeStruct((n_tok * ROW_TILES, LANES), F32),
                   jax.ShapeDtypeStruct((n_tok, LANES), F32)],
        compiler_params=_params(1),
        name="o_proj_tail",
    )(x, o_t, mods, wo, gf, wr, br)


def _bpost_kernel(x_ref, o_ref, mod_ref, wo_ref, gf_ref, wr_ref, br_ref, x1_ref, h2_ref, lg_ref):
    y = jnp.dot(o_ref[...], wo_ref[...], preferred_element_type=F32)
    x1 = x_ref[...] + mod_ref[0] * y
    _tail(x1, gf_ref, mod_ref[1], mod_ref[2], wr_ref, br_ref, x1_ref, h2_ref, lg_ref)


def _bpost_call(x, o, mods, wo, gf, wr, br, tiles_per_mod):
    n_tok = x.shape[0]
    tt = min(n_tok, TOKEN_TILE)
    xspec, mspec = _tok_specs(tt, mods, tiles_per_mod)
    const = lambda i: (0, 0)
    return pl.pallas_call(
        _bpost_kernel,
        grid=(n_tok // tt,),
        in_specs=[xspec, xspec, mspec, pl.BlockSpec((D_MODEL, D_MODEL), const), pl.BlockSpec((1, D_MODEL), const),
                  pl.BlockSpec((2, D_MODEL, LANES), lambda i: (0, 0, 0)), pl.BlockSpec((1, LANES), const)],
        out_specs=[xspec, pl.BlockSpec((tt * ROW_TILES, LANES), lambda i: (i, 0)),
                   pl.BlockSpec((tt, LANES), lambda i: (i, 0))],
        out_shape=[jax.ShapeDtypeStruct((n_tok, D_MODEL), F32),
                   jax.ShapeDtypeStruct((n_tok * ROW_TILES, LANES), F32),
                   jax.ShapeDtypeStruct((n_tok, LANES), F32)],
        compiler_params=_params(1),
        name="o_proj_tail",
    )(x, o, mods, wo, gf, wr, br)


def _final_kernel(x_ref, mod_ref, g_ref, y_out):
    y_out[...] = _norm_mod(x_ref[...], g_ref[...], mod_ref[0], mod_ref[1])


def _final_call(x, mods, g, tiles_per_mod):
    n_tok = x.shape[0]
    tt = min(n_tok, TOKEN_TILE)
    xspec, mspec = _tok_specs(tt, mods, tiles_per_mod)
    return pl.pallas_call(
        _final_kernel,
        grid=(n_tok // tt,),
        in_specs=[xspec, mspec, pl.BlockSpec((1, D_MODEL), lambda i: (0, 0))],
        out_specs=xspec,
        out_shape=jax.ShapeDtypeStruct((n_tok, D_MODEL), F32),
        compiler_params=_params(1),
        name="final_norm",
    )(x, mods, g)


def _route_kernel(lg_ref, idx_ref, gate_ref, rank_ref, cnt_ref, carry, *, tr):
    i = pl.program_id(0)

    @pl.when(i == 0)
    def _():
        carry[...] = jnp.zeros_like(carry)

    lane = lax.broadcasted_iota(jnp.int32, (tr, LANES), 1)
    lanef = lane.astype(F32)
    logit = jnp.where(lane < N_EXPERTS, lg_ref[...], -jnp.inf)
    r_i = lax.broadcasted_iota(jnp.int32, (tr, tr), 0)
    c_i = lax.broadcasted_iota(jnp.int32, (tr, tr), 1)
    earlier = jnp.where(c_i < r_i, 1.0, 0.0).astype(BF16)
    base = carry[...]
    idx_out = jnp.zeros((tr, LANES), F32)
    rank_out = jnp.zeros((tr, LANES), F32)
    val_out = jnp.zeros((tr, LANES), F32)
    top0 = None
    for k in range(TOP_K):
        m = jnp.max(logit, axis=1, keepdims=True)
        sel = jnp.min(jnp.where(logit == m, lanef, float(LANES)), axis=1, keepdims=True)
        onehot = lanef == sel
        ohf = jnp.where(onehot, 1.0, 0.0)
        before = jnp.dot(earlier, ohf.astype(BF16), preferred_element_type=F32)
        rank = jnp.sum(jnp.where(onehot, before + base, 0.0), axis=1, keepdims=True)
        base = base + jnp.sum(ohf, axis=0, keepdims=True)
        logit = jnp.where(onehot, -jnp.inf, logit)
        if k == 0:
            top0 = m
        idx_out = jnp.where(lane == k, sel, idx_out)
        rank_out = jnp.where(lane == k, rank, rank_out)
        val_out = jnp.where(lane == k, jnp.exp(m - top0), val_out)
    carry[...] = base
    cnt_ref[...] = base
    idx_ref[...] = idx_out
    rank_ref[...] = rank_out
    gate_ref[...] = val_out / jnp.sum(val_out, axis=1, keepdims=True)


def _route_call(logits):
    n_tok = logits.shape[0]
    tr = min(n_tok, TOKEN_TILE)
    spec = pl.BlockSpec((tr, LANES), lambda i: (i, 0))
    sds = jax.ShapeDtypeStruct((n_tok, LANES), F32)
    return pl.pallas_call(
        functools.partial(_route_kernel, tr=tr),
        grid=(n_tok // tr,),
        in_specs=[spec],
        out_specs=[spec, spec, spec, pl.BlockSpec((1, LANES), lambda i: (0, 0))],
        out_shape=[sds, sds, sds, jax.ShapeDtypeStruct((1, LANES), F32)],
        scratch_shapes=[pltpu.VMEM((1, LANES), F32)],
        compiler_params=_params(1),
        name="route",
    )(logits)


def _row_copy(src_hbm, src_row, dst_hbm, dst_row, sem):
    return pltpu.make_async_copy(src_hbm.at[pl.ds(pl.multiple_of(src_row, ROW_TILES), ROW_TILES), :],
                                 dst_hbm.at[pl.ds(pl.multiple_of(dst_row, ROW_TILES), ROW_TILES), :], sem)


def _dispatch_kernel(dst_ref, h_ref, xs_in_hbm, xs_hbm, sem, *, chunk):
    del xs_in_hbm

    def issue(g, carry):
        tok = g * (DMA_UNROLL // TOP_K)
        for u in range(DMA_UNROLL):
            copy = _row_copy(h_ref, (tok + u // TOP_K) * ROW_TILES, xs_hbm, dst_ref[0, g * DMA_UNROLL + u], sem)
            copy.start(priority=u % 2)
        return carry

    def drain(g, carry):
        for u in range(DMA_UNROLL):
            _row_copy(h_ref, 0, xs_hbm, 0, sem).wait()
        return carry

    lax.fori_loop(0, chunk // DMA_UNROLL, issue, 0)
    lax.fori_loop(0, chunk // DMA_UNROLL, drain, 0)


def _dispatch_call(h2_tiles, dst_rows, xs):
    n_asg = dst_rows.shape[0]
    chunk = min(n_asg, DISPATCH_CHUNK)
    any_spec = pl.BlockSpec(memory_space=pl.ANY)
    return pl.pallas_call(
        functools.partial(_dispatch_kernel, chunk=chunk),
        grid=(n_asg // chunk,),
        in_specs=[pl.BlockSpec((None, 1, chunk), lambda i: (i, 0, 0), memory_space=pltpu.SMEM),
                  pl.BlockSpec((chunk // TOP_K * ROW_TILES, LANES), lambda i: (i, 0)), any_spec],
        out_specs=any_spec,
        out_shape=jax.ShapeDtypeStruct(xs.shape, F32),
        scratch_shapes=[pltpu.SemaphoreType.DMA(())],
        input_output_aliases={2: 0},
        compiler_params=_params(1),
        name="moe_dispatch",
    )(dst_rows.reshape(n_asg // chunk, 1, chunk), h2_tiles, xs)


def _moe_kernel(be_ref, nv_ref, x_ref, wgu_ref, bgu_ref, wd_ref, bd_ref, o_ref, wgu_bf, wd_bf, *, tm):
    i = pl.program_id(0)
    active = i < nv_ref[0]
    fresh = jnp.logical_or(i == 0, be_ref[i] != be_ref[jnp.maximum(i - 1, 0)])

    @pl.when(jnp.logical_and(active, fresh))
    def _():
        for c in range(2 * D_FF // FF_CHUNK):
            wgu_bf[:, c * FF_CHUNK:(c + 1) * FF_CHUNK] = wgu_ref[:, c * FF_CHUNK:(c + 1) * FF_CHUNK].astype(BF16)
        for c in range(D_MODEL // FF_CHUNK):
            wd_bf[:, c * FF_CHUNK:(c + 1) * FF_CHUNK] = wd_ref[:, c * FF_CHUNK:(c + 1) * FF_CHUNK].astype(BF16)

    @pl.when(active)
    def _():
        x = _load_row_tiles(x_ref, tm).astype(BF16)
        acc = None
        for c in range(D_FF // FF_CHUNK):
            lo = c * FF_CHUNK
            glu = jnp.dot(x, wgu_bf[:, lo:lo + FF_CHUNK], preferred_element_type=F32) + bgu_ref[:, lo:lo + FF_CHUNK]
            lin = (jnp.dot(x, wgu_bf[:, D_FF + lo:D_FF + lo + FF_CHUNK], preferred_element_type=F32)
                   + bgu_ref[:, D_FF + lo:D_FF + lo + FF_CHUNK])
            glu = jnp.minimum(glu, SWIGLU_LIMIT)
            lin = jnp.clip(lin, -SWIGLU_LIMIT, SWIGLU_LIMIT)
            act = glu * jax.nn.sigmoid(SWIGLU_ALPHA * glu) * (lin + 1.0)
            part = jnp.dot(act.astype(BF16), wd_bf[lo:lo + FF_CHUNK, :], preferred_element_type=F32)
            acc = part if acc is None else acc + part
        _store_row_tiles(o_ref, acc + bd_ref[...])

    @pl.when(jnp.logical_not(active))
    def _():
        o_ref[...] = jnp.zeros_like(o_ref)


def _moe_call(xs, blk_expert, n_valid, layer, wgu, bgu, wd, bd, tm):
    n_blocks = blk_expert.shape[0]
    rspec = pl.BlockSpec((tm * ROW_TILES, LANES), lambda i, be, nv: (i, 0))
    emap = lambda i, be, nv: (layer, be[i], 0, 0)
    grid_spec = pltpu.PrefetchScalarGridSpec(
        num_scalar_prefetch=2,
        grid=(n_blocks,),
        in_specs=[rspec,
                  pl.BlockSpec((None, None, D_MODEL, 2 * D_FF), emap),
                  pl.BlockSpec((None, None, 1, 2 * D_FF), emap),
                  pl.BlockSpec((None, None, D_FF, D_MODEL), emap),
                  pl.BlockSpec((None, None, 1, D_MODEL), emap)],
        out_specs=rspec,
        scratch_shapes=[pltpu.VMEM((D_MODEL, 2 * D_FF), BF16), pltpu.VMEM((D_FF, D_MODEL), BF16)],
    )
    return pl.pallas_call(
        functools.partial(_moe_kernel, tm=tm),
        grid_spec=grid_spec,
        out_shape=jax.ShapeDtypeStruct((n_blocks * tm * ROW_TILES, LANES), F32),
        compiler_params=_params(1),
        name="moe_experts",
    )(blk_expert, n_valid, xs, wgu, bgu, wd, bd)


def _combine_kernel(cur_ref, nxt_ref, x_ref, gate_ref, g2_ref, out_hbm, x2_ref, buf, sems, *, tt, n_tiles):
    i = pl.program_id(0)
    n_copy = tt * TOP_K

    def gather(rows_ref, slot, wait):
        def body(g, carry):
            tok = g * (DMA_UNROLL // TOP_K)
            for u in range(DMA_UNROLL):
                if wait:
                    _row_copy(out_hbm, 0, buf.at[slot], 0, sems.at[slot]).wait()
                else:
                    dst = ((u % TOP_K) * tt + tok + u // TOP_K) * ROW_TILES
                    copy = _row_copy(out_hbm, rows_ref[0, g * DMA_UNROLL + u], buf.at[slot], dst, sems.at[slot])
                    copy.start(priority=u % 2)
            return carry
        lax.fori_loop(0, n_copy // DMA_UNROLL, body, 0)

    slot = i % 2

    @pl.when(i == 0)
    def _():
        gather(cur_ref, 0, False)

    @pl.when(i + 1 < n_tiles)
    def _():
        gather(nxt_ref, 1 - slot, False)

    gather(cur_ref, slot, True)
    gates = gate_ref[...]
    gate_k = [jnp.broadcast_to(gates[:, k:k + 1], (tt, LANES)) for k in range(TOP_K)]
    rows = buf.at[slot]
    for j in range(ROW_TILES):
        cols = slice(j * LANES, (j + 1) * LANES)
        y = None
        for k in range(TOP_K):
            term = gate_k[k] * rows[pl.ds(k * tt * ROW_TILES + j, tt, stride=ROW_TILES), :]
            y = term if y is None else y + term
        x2_ref[:, cols] = x_ref[:, cols] + g2_ref[0, :, cols] * y


def _combine_call(x1, gate, g2, out_tiles, pos_rows, tiles_per_mod_512):
    n_tok = x1.shape[0]
    tt = min(n_tok, COMBINE_TILE)
    n_tiles = n_tok // tt
    rows_per_mod = g2.shape[2]
    tiles_per_mod = tiles_per_mod_512 * (TOKEN_TILE // tt) if rows_per_mod == 1 else 1
    if rows_per_mod == 1:
        g2spec = pl.BlockSpec((None, 1, 1, D_MODEL), lambda i: (i // tiles_per_mod, 0, 0, 0))
    else:
        g2spec = pl.BlockSpec((None, 1, tt, D_MODEL), lambda i: (0, 0, i, 0))
    pos3 = pos_rows.reshape(n_tiles, 1, tt * TOP_K)
    pspec = lambda f: pl.BlockSpec((None, 1, tt * TOP_K), f, memory_space=pltpu.SMEM)
    return pl.pallas_call(
        functools.partial(_combine_kernel, tt=tt, n_tiles=n_tiles),
        grid=(n_tiles,),
        in_specs=[pspec(lambda i: (i, 0, 0)), pspec(lambda i: (jnp.minimum(i + 1, n_tiles - 1), 0, 0)),
                  pl.BlockSpec((tt, D_MODEL), lambda i: (i, 0)), pl.BlockSpec((tt, LANES), lambda i: (i, 0)),
                  g2spec, pl.BlockSpec(memory_space=pl.ANY)],
        out_specs=pl.BlockSpec((tt, D_MODEL), lambda i: (i, 0)),
        out_shape=jax.ShapeDtypeStruct((n_tok, D_MODEL), F32),
        scratch_shapes=[pltpu.VMEM((2, tt * TOP_K * ROW_TILES, LANES), F32), pltpu.SemaphoreType.DMA((2,))],
        compiler_params=_params(1),
        name="moe_combine",
    )(pos3, pos3, x1, gate, g2, out_tiles)


def _moe(x1, h2_tiles, logits, g2, xs, layer, experts, tm, tiles_per_mod):
    n_tok = x1.shape[0]
    idx_f, gate_f, rank_f, cnt_f = _route_call(logits)
    idx = idx_f[:, :TOP_K].astype(jnp.int32)
    rank = rank_f[:, :TOP_K].astype(jnp.int32)
    counts = cnt_f[0, :N_EXPERTS].astype(jnp.int32)
    padded = (counts + tm - 1) // tm * tm
    pad_end = jnp.cumsum(padded)
    pad_start = pad_end - padded
    pos_rows = ((pad_start[idx] + rank) * ROW_TILES).reshape(-1)
    n_blocks = n_tok * TOP_K // tm + N_EXPERTS
    blk_start = jnp.arange(n_blocks, dtype=jnp.int32) * tm
    blk_expert = jnp.minimum(jnp.sum(pad_end[None, :] <= blk_start[:, None], axis=1), N_EXPERTS - 1).astype(jnp.int32)
    n_valid = (pad_end[-1] // tm).astype(jnp.int32).reshape(1)
    xs = _dispatch_call(h2_tiles, pos_rows, xs)
    out = _moe_call(xs, blk_expert, n_valid, layer, *experts, tm)
    return _combine_call(x1, gate_f, g2, out, pos_rows, tiles_per_mod), xs


def _cumsum_kernel(x_ref, o_ref, carry, *, tc):
    j = pl.program_id(1)

    @pl.when(j == 0)
    def _():
        carry[...] = jnp.zeros_like(carry)

    r_i = lax.broadcasted_iota(jnp.int32, (tc, tc), 0)
    c_i = lax.broadcasted_iota(jnp.int32, (tc, tc), 1)
    upto = jnp.where(r_i <= c_i, 1.0, 0.0)
    cs = jnp.dot(x_ref[...], upto, preferred_element_type=F32, precision=HIGHEST) + carry[...]
    o_ref[...] = cs
    carry[...] = cs[:, tc - 1:tc]


def _cumsum_call(x):
    rows, tk = x.shape
    tc = LANES
    tr = min(rows, LANES)
    spec = pl.BlockSpec((tr, tc), lambda r, j: (r, j))
    return pl.pallas_call(
        functools.partial(_cumsum_kernel, tc=tc),
        grid=(rows // tr, tk // tc),
        in_specs=[spec],
        out_specs=spec,
        out_shape=jax.ShapeDtypeStruct(x.shape, F32),
        scratch_shapes=[pltpu.VMEM((tr, 1), F32)],
        compiler_params=_params(2),
        name="logf_cumsum",
    )(x)


ATTN_K_ROWS = LANES
ATTN_V_ROWS = HEAD_DIM + 16


def _attn_kernel(q_ref, k_ref, v_ref, o_ref, acc_sc, *, tq, tk, pos0, nk):
    qi = pl.program_id(2)
    q_first = pos0 + qi * tq
    n_full = jnp.minimum(nk, (q_first + 1) // tk)
    n_end = jnp.minimum(nk, (q_first + tq - 1) // tk + 1)
    acc_sc[...] = jnp.zeros_like(acc_sc)
    width = KV_GROUP * tq

    def block(ki, m, masked):
        s = jnp.dot(k_ref[ki], q_ref[...], preferred_element_type=F32)
        if masked:
            key = ki * tk + lax.broadcasted_iota(jnp.int32, (tk, width), 0)
            qry = q_first + (lax.broadcasted_iota(jnp.int32, (tk, width), 1) & (tq - 1))
            s = jnp.where(key <= qry, s, NEG_INF)
        m_new = jnp.maximum(m, jnp.max(s, axis=0, keepdims=True))
        alpha = jnp.exp(m - m_new)
        p = jnp.exp(s - m_new).astype(BF16)
        acc_sc[...] = alpha * acc_sc[...] + jnp.dot(v_ref[ki], p, preferred_element_type=F32)
        return m_new

    m = jnp.full((1, width), NEG_INF, F32)
    m = lax.fori_loop(0, n_full, lambda ki, c: block(ki, c, False), m)
    lax.fori_loop(n_full, n_end, lambda ki, c: block(ki, c, True), m)
    a = acc_sc[...]
    o_ref[...] = (a[:HEAD_DIM] / a[HEAD_DIM:HEAD_DIM + 1]).astype(BF16)


def _attn_call(q_aug, k_aug, v_aug, pos0):
    bsz, _, nq, _, width = q_aug.shape
    nk, tk = k_aug.shape[2], k_aug.shape[3]
    tq = width // KV_GROUP
    assert tq & (tq - 1) == 0
    return pl.pallas_call(
        functools.partial(_attn_kernel, tq=tq, tk=tk, pos0=pos0, nk=nk),
        grid=(bsz, N_KV_HEADS, nq),
        in_specs=[pl.BlockSpec((None, None, None, ATTN_K_ROWS, width), lambda b, h, qi: (b, h, qi, 0, 0)),
                  pl.BlockSpec((None, None, nk, tk, ATTN_K_ROWS), lambda b, h, qi: (b, h, 0, 0, 0)),
                  pl.BlockSpec((None, None, nk, ATTN_V_ROWS, tk), lambda b, h, qi: (b, h, 0, 0, 0))],
        out_specs=pl.BlockSpec((None, None, None, HEAD_DIM, width), lambda b, h, qi: (b, h, qi, 0, 0)),
        out_shape=jax.ShapeDtypeStruct((bsz, N_KV_HEADS, nq, HEAD_DIM, width), BF16),
        scratch_shapes=[pltpu.VMEM((ATTN_V_ROWS, width), F32)],
        compiler_params=_params(3),
        name="forget_attention",
    )(q_aug, k_aug, v_aug)


def _split3(c):
    def top(x):
        bits = lax.bitcast_convert_type(x, jnp.uint32) & jnp.uint32(0xFFFF0000)
        return lax.bitcast_convert_type(bits, F32)
    hi = top(c)
    mid = top(c - hi)
    lo = top(c - hi - mid)
    return hi.astype(BF16), mid.astype(BF16), lo.astype(BF16)


def _attn_kv_operands(k_all, v_all, cum, tk):
    bsz, seq_k = k_all.shape[:2]
    nk = seq_k // tk
    kt = k_all.astype(BF16).transpose(0, 2, 1, 3)
    ck = jnp.stack(_split3(-cum.reshape(bsz, N_KV_HEADS, KV_GROUP, seq_k)), axis=3)
    ck = ck.reshape(bsz, N_KV_HEADS, 3 * KV_GROUP, seq_k).transpose(0, 1, 3, 2)
    ones = jnp.ones((bsz, N_KV_HEADS, seq_k, 3), BF16)
    zpad = jnp.zeros((bsz, N_KV_HEADS, seq_k, ATTN_K_ROWS - HEAD_DIM - 3 - 3 * KV_GROUP), BF16)
    k_aug = jnp.concatenate([kt, ones, ck, zpad], axis=3).reshape(bsz, N_KV_HEADS, nk, tk, ATTN_K_ROWS)
    vt = v_all.astype(BF16).transpose(0, 2, 3, 1)
    v_aug = jnp.concatenate([vt, jnp.ones((bsz, N_KV_HEADS, 1, seq_k), BF16),
                             jnp.zeros((bsz, N_KV_HEADS, ATTN_V_ROWS - HEAD_DIM - 1, seq_k), BF16)], axis=2)
    v_aug = v_aug.reshape(bsz, N_KV_HEADS, ATTN_V_ROWS, nk, tk).transpose(0, 1, 3, 2, 4)
    return k_aug, v_aug


def _attn_q_operand(q, cum_q, tq):
    bsz, seq = q.shape[:2]
    nq, width = seq // tq, KV_GROUP * tq
    lead = (bsz, N_KV_HEADS, nq)
    qt = q.reshape(bsz, nq, tq, N_KV_HEADS, KV_GROUP, HEAD_DIM).transpose(0, 3, 1, 5, 4, 2).reshape(*lead, HEAD_DIM, width)
    to_lanes = lambda c: c.reshape(bsz, N_KV_HEADS, KV_GROUP, nq, tq).transpose(0, 1, 3, 2, 4).reshape(*lead, 1, width)
    cq = [to_lanes(c) for c in _split3(cum_q)]
    sel = jnp.repeat(jnp.repeat(jnp.eye(KV_GROUP, dtype=BF16), 3, axis=0), tq, axis=1)
    sel = jnp.broadcast_to(sel, (*lead, 3 * KV_GROUP, width))
    zpad = jnp.zeros((*lead, ATTN_K_ROWS - HEAD_DIM - 3 - 3 * KV_GROUP, width), BF16)
    return jnp.concatenate([qt, *cq, sel, zpad], axis=3)


def _attn_untranspose(o_t, tq):
    bsz, _, nq = o_t.shape[:3]
    o = o_t.reshape(bsz, N_KV_HEADS, nq, HEAD_DIM, KV_GROUP, tq).transpose(0, 2, 5, 1, 4, 3)
    return o.reshape(bsz * nq * tq, D_MODEL)


def _split_mods(m, n, expand_t):
    bsz = m.shape[0]
    m = m.reshape(bsz, n, 1, D_MODEL)
    if expand_t:
        m = jnp.broadcast_to(m.transpose(1, 0, 2, 3), (n, bsz, expand_t, D_MODEL)).reshape(1, n, bsz * expand_t, D_MODEL)
    return m


def _trunk(x, ada, ada_kv, ada_final, pool_hist, kv_past, pos0, w):
    bsz, seq, _ = x.shape
    n_tok = bsz * seq
    flat_small = seq < TOKEN_TILE
    expand_t = seq if flat_small else 0
    tiles_per_mod = 1 if flat_small else seq // TOKEN_TILE
    tm = MOE_TILE if n_tok * TOP_K >= 64 * MOE_TILE else 128
    hist16 = jnp.pad(pool_hist, ((0, 0), (0, 0), (1, 0), (0, 0)))
    xs = jnp.zeros(((n_tok * TOP_K + N_EXPERTS * tm) * ROW_TILES, LANES), F32)
    new_pool = []
    xf = x.reshape(n_tok, D_MODEL)
    k4 = v4 = logf_new = k_aug = v_aug = cum_q = None
    for layer in range(DEPTH):
        lw = w["layers"][layer]
        m6 = ada[layer]
        if layer < N_A_LAYERS:
            mods = _split_mods(m6[:, :5 * D_MODEL], 5, 0)
            x1, h2, logits, hout = _pool_call(xf.reshape(bsz, seq, D_MODEL), hist16[layer], mods, lw["gm"], lw["gf"],
                                              lw["w_pool"], lw["pool_scale"], lw["wr"], lw["br"], pos0)
            new_pool.append(hout[:, 1:])
            x1 = x1.reshape(n_tok, D_MODEL)
        else:
            mods = _split_mods(m6[:, :5 * D_MODEL], 5, expand_t)
            tq = min(seq, ATTN_TQ)
            if flat_small:
                q = _bpre_call(xf, mods[:, 0:2], lw["gm"], lw["wq"], tiles_per_mod)
                o_t = _attn_call(_attn_q_operand(q.reshape(bsz, seq, D_MODEL), cum_q, tq), k_aug, v_aug, pos0)
                x1, h2, logits = _bpost_call(xf, _attn_untranspose(o_t, tq), mods[:, 2:5], lw["wo"], lw["gf"],
                                             lw["wr"], lw["br"], tiles_per_mod)
            else:
                q_aug = _q_aug_call(xf, mods[:, 0:2], lw["gm"], lw["wqt"], cum_q, tq, tiles_per_mod)
                o_t = _attn_call(q_aug, k_aug, v_aug, pos0)
                x1, h2, logits = _bpost_t_call(xf, o_t, mods[:, 2:5], lw["wo"], lw["gf"], lw["wr"], lw["br"],
                                               tiles_per_mod)
        g2 = _split_mods(m6[:, 5 * D_MODEL:], 1, expand_t)
        xf, xs = _moe(x1, h2, logits, g2, xs, layer, w["experts"], tm, tiles_per_mod)
        if layer == N_A_LAYERS - 1:
            mods_kv = _split_mods(ada_kv, 2, expand_t)
            k_new, v_new, lf = _kv_call(xf, mods_kv, w["g_kv"], w["wk"], w["wv"], w["wf"], w["bf"], tiles_per_mod)
            logf_new = lf[:, :N_HEADS].reshape(bsz, seq, N_HEADS)
            k4 = k_new.reshape(bsz, seq, N_KV_HEADS, HEAD_DIM)
            v4 = v_new.reshape(bsz, seq, N_KV_HEADS, HEAD_DIM)
            if kv_past is None:
                k_all, v_all, logf_all = k4, v4, logf_new
            else:
                k_all = jnp.concatenate([kv_past[0], k4], axis=1)
                v_all = jnp.concatenate([kv_past[1], v4], axis=1)
                logf_all = jnp.concatenate([kv_past[2], logf_new], axis=1)
            seq_k = k_all.shape[1]
            seq_kp = -(-seq_k // LANES) * LANES
            padk = ((0, 0), (0, seq_kp - seq_k), (0, 0), (0, 0))
            lf_t = jnp.pad(logf_all, padk[:3]).transpose(0, 2, 1).reshape(bsz * N_HEADS, seq_kp)
            cum = _cumsum_call(lf_t).reshape(bsz, N_HEADS, seq_kp)
            cum_q = cum[:, :, seq_k - seq:seq_k]
            tk = ATTN_TK if (seq >= ATTN_TQ and seq_kp % ATTN_TK == 0) else seq_kp
            k_aug, v_aug = _attn_kv_operands(jnp.pad(k_all, padk), jnp.pad(v_all, padk), cum, tk)
    mods_f = _split_mods(ada_final, 2, expand_t)
    y_out = _final_call(xf, mods_f, w["g_final"], tiles_per_mod)
    return (y_out.reshape(bsz, seq, D_MODEL), k4, v4, logf_new, jnp.stack(new_pool, axis=0))


def _pad_lanes(a, axis):
    pad = [(0, 0)] * a.ndim
    pad[axis] = (0, LANES - a.shape[axis])
    return jnp.pad(a, pad)


def _router_operand(w):
    w = _pad_lanes(w, 1)
    hi = _top_bits(w)
    return jnp.stack([hi.astype(BF16), (w - hi).astype(BF16)])


def kernel(x_prompt, x_sample, cache_k, cache_v, cache_logf, state_pool, c_prompt, c_sample, norm_mix_g, norm_ffn_g, w_ada, b_ada, w_pool, pool_scale, norm_kv_g, w_ada_kv, b_ada_kv, w_kv, b_forget, w_q, w_o, w_router, b_router, w_gate_up, b_gate_up, w_down, b_down, norm_final_g, w_ada_final, b_ada_final):
    bp = c_prompt.shape[0]
    c_all = jnp.concatenate([c_prompt, c_sample], axis=0)
    ada = _ada_call(c_all, w_ada, b_ada)
    w_small = jnp.stack([w_ada_kv, w_ada_final])
    b_small = jnp.stack([b_ada_kv, b_ada_final])
    ada_small = _ada_call(c_all, w_small, b_small)

    layers = []
    for layer in range(DEPTH):
        lw = dict(gm=norm_mix_g[layer][None], gf=norm_ffn_g[layer][None],
                  wr=_router_operand(w_router[layer]), br=_pad_lanes(b_router[layer][None], 1),
                  )
        if layer < N_A_LAYERS:
            lw.update(w_pool=w_pool[layer].astype(BF16), pool_scale=pool_scale[layer][None])
        else:
            j = layer - N_A_LAYERS
            lw.update(wq=w_q[j].astype(BF16), wqt=w_q[j].T.astype(BF16), wo=w_o[j].astype(BF16))
        layers.append(lw)
    w = dict(layers=layers, g_kv=norm_kv_g[None], g_final=norm_final_g[None],
             wk=w_kv[:, :KV_WIDTH].astype(BF16), wv=w_kv[:, KV_WIDTH:2 * KV_WIDTH].astype(BF16),
             wf=_pad_lanes(w_kv[:, 2 * KV_WIDTH:], 1), bf=_pad_lanes(b_forget[None], 1),
             experts=(w_gate_up, b_gate_up[:, :, None, :], w_down, b_down[:, :, None, :]))

    zero_hist = jnp.zeros((N_A_LAYERS, bp, POOL_HIST, D_MODEL), F32)
    y_p, k_p, v_p, lf_p, pool_p = _trunk(x_prompt, ada[:, :bp], ada_small[0, :bp], ada_small[1, :bp],
                                         zero_hist, None, 0, w)
    y_s, k_s, v_s, lf_s, pool_s = _trunk(x_sample, ada[:, bp:], ada_small[0, bp:], ada_small[1, bp:],
                                         state_pool, (cache_k, cache_v, cache_logf), cache_k.shape[1], w)
    return (y_p, y_s, k_p, v_p, lf_p, pool_p, k_s, v_s, lf_s, pool_s)
```
